```python
import jax, jax.numpy as jnp
from jax import lax
import numpy as np

D_MODEL = 1024
BATCH = 2
SEQ = 8192
DEPTH = 4
DEC_BATCH = 32
DEC_SEQ = 8
PAST_LEN = 8192
PAGE_SIZE = 128

N_A_LAYERS = DEPTH // 2
N_B_LAYERS = DEPTH - N_A_LAYERS
D_CONV = D_MODEL
CONV_W = 3
N_HEADS = 16
HEAD_DIM = 64
D_ATT = N_HEADS * HEAD_DIM
D_PLE = 256
Q_BLOCK = 128
EPS = 1e-6
NEG_INF = -1e30

kernel_name = "yoco_shortconv_forgetting_attention_step"


def rmsnorm(x, g):
    xf = x.astype(jnp.float32)
    y = xf * lax.rsqrt(jnp.mean(xf * xf, axis=-1, keepdims=True) + EPS)
    return (y * g.astype(jnp.float32)).astype(x.dtype)


def short_conv_layer(x, conv_state, g, w_in, conv_w, w_out):
    h = rmsnorm(x, g)
    b_gate, c_gate, xin, z = jnp.split(h @ w_in, 4, axis=-1)
    u = c_gate * xin
    u_pad = jnp.concatenate([conv_state.astype(u.dtype), u], axis=1)
    T = u.shape[1]
    conv = conv_w[0] * u_pad[:, 0:T]
    for j in range(1, CONV_W):
        conv = conv + conv_w[j] * u_pad[:, j:j + T]
    y = b_gate * conv * jax.nn.silu(z)
    return x + y @ w_out, u_pad[:, -(CONV_W - 1):]


def shared_kv(x, g, w_kv, b_f):
    Bg, T = x.shape[:2]
    h = rmsnorm(x, g)
    k, v, fl = jnp.split(h @ w_kv, [D_ATT, 2 * D_ATT], axis=-1)
    k = k.reshape(Bg, T, N_HEADS, HEAD_DIM)
    v = v.reshape(Bg, T, N_HEADS, HEAD_DIM)
    logf = jax.nn.log_sigmoid(fl.astype(jnp.float32) + b_f.astype(jnp.float32))
    return k, v, logf


def forgetting_attn_layer(x, g, w_in, w_out, attend):
    Bg, T = x.shape[:2]
    h = rmsnorm(x, g)
    q, z = jnp.split(h @ w_in, 2, axis=-1)
    o = attend(q.reshape(Bg, T, N_HEADS, HEAD_DIM))
    y = o.reshape(Bg, T, D_ATT) * jax.nn.silu(z)
    return x + y @ w_out


def prompt_attention(q, k, v, logf):
    B, S, H, dh = q.shape
    scale = HEAD_DIM ** -0.5
    c = jnp.cumsum(logf, axis=1)
    ck = jnp.swapaxes(c, 1, 2)
    kpos = jnp.arange(S)
    nb = S // Q_BLOCK

    def block(args):
        qb, cqb, i = args
        qpos = i * Q_BLOCK + jnp.arange(Q_BLOCK)
        s = jnp.einsum('bqhd,bkhd->bhqk', qb, k).astype(jnp.float32) * scale
        s = s + jnp.swapaxes(cqb, 1, 2)[..., None] - ck[:, :, None, :]
        s = jnp.where(kpos[None, :] <= qpos[:, None], s, NEG_INF)
        p = jax.nn.softmax(s, axis=-1)
        return jnp.einsum('bhqk,bkhd->bqhd', p.astype(v.dtype), v)

    qb = q.reshape(B, nb, Q_BLOCK, H, dh).transpose(1, 0, 2, 3, 4)
    cqb = c.reshape(B, nb, Q_BLOCK, H).transpose(1, 0, 2, 3)
    out = lax.map(block, (qb, cqb, jnp.arange(nb)))
    return out.transpose(1, 0, 2, 3, 4).reshape(B, S, H, dh)


def sample_attention(q, k_new, v_new, logf_new, k_past, v_past, logf_past):
    T = q.shape[1]
    scale = HEAD_DIM ** -0.5
    cp = jnp.cumsum(logf_past, axis=1)
    cn = jnp.cumsum(logf_new, axis=1)
    rest = cp[:, -1:, :] - cp
    cn_q = jnp.swapaxes(cn, 1, 2)[..., None]
    s_past = jnp.einsum('bqhd,bkhd->bhqk', q, k_past).astype(jnp.float32) * scale
    s_past = s_past + cn_q + jnp.swapaxes(rest, 1, 2)[:, :, None, :]
    s_new = jnp.einsum('bqhd,bkhd->bhqk', q, k_new).astype(jnp.float32) * scale
    s_new = s_new + cn_q - jnp.swapaxes(cn, 1, 2)[:, :, None, :]
    causal = jnp.arange(T)[None, :] <= jnp.arange(T)[:, None]
    s_new = jnp.where(causal, s_new, NEG_INF)
    P = k_past.shape[1]
    p = jax.nn.softmax(jnp.concatenate([s_past, s_new], axis=-1), axis=-1)
    o = jnp.einsum('bhqk,bkhd->bqhd', p[..., :P].astype(v_past.dtype), v_past)
    o = o + jnp.einsum('bhqk,bkhd->bqhd', p[..., P:].astype(v_new.dtype), v_new)
    return o.astype(q.dtype)


def trunk(x, p, conv_state, make_attend, norm_a, w_in_a, conv_w_a, w_out_a,
          norm_kv, w_kv, b_f, norm_b, w_in_b, w_out_b, w_ple_proj, w_ple_gate, norm_f):
    new_conv = []
    k = v = logf = None
    attend = None
    for i in range(DEPTH):
        if i < N_A_LAYERS:
            x, st = short_conv_layer(x, conv_state[i], norm_a[i], w_in_a[i], conv_w_a[i], w_out_a[i])
            new_conv.append(st)
        else:
            if i == N_A_LAYERS:
                k, v, logf = shared_kv(x, norm_kv, w_kv, b_f)
                attend = make_attend(k, v, logf)
            j = i - N_A_LAYERS
            x = forgetting_attn_layer(x, norm_b[j], w_in_b[j], w_out_b[j], attend)
        x = x + jax.nn.sigmoid(x @ w_ple_gate[i]) * (p[i].astype(x.dtype) @ w_ple_proj[i])
    y = rmsnorm(x, norm_f)
    return y, jnp.stack(new_conv), k, v, logf.astype(x.dtype)


def setup_inputs(seed: int = 0) -> dict:
    key = jax.random.key(seed)
    ks = jax.random.split(key, 24)
    f32 = jnp.float32
    n_pages = PAST_LEN // PAGE_SIZE
    n_pool = (5 * DEC_BATCH * n_pages) // 4
    nrm = lambda k, shape, s: jax.random.normal(k, shape, f32) * s

    x_prompt = nrm(ks[0], (BATCH, SEQ, D_MODEL), 1.0)
    x_sample = nrm(ks[1], (DEC_BATCH, DEC_SEQ, D_MODEL), 1.0)
    p_prompt = nrm(ks[2], (DEPTH, BATCH, SEQ, D_PLE), 1.0)
    p_sample = nrm(ks[3], (DEPTH, DEC_BATCH, DEC_SEQ, D_PLE), 1.0)
    state_conv = nrm(ks[4], (N_A_LAYERS, DEC_BATCH, CONV_W - 1, D_CONV), 1.0)
    cache_k = nrm(ks[5], (n_pool, PAGE_SIZE, N_HEADS, HEAD_DIM), 1.0)
    cache_v = nrm(ks[6], (n_pool, PAGE_SIZE, N_HEADS, HEAD_DIM), 1.0)
    head_bias = 2.0 + 4.0 * jax.random.uniform(ks[7], (N_HEADS,), f32)
    cache_logf = jax.nn.log_sigmoid(head_bias + nrm(ks[8], (n_pool, PAGE_SIZE, N_HEADS), 1.0))
    page_table = jax.random.permutation(ks[9], n_pool)[:DEC_BATCH * n_pages].reshape(
        DEC_BATCH, n_pages).astype(jnp.int32)

    norm_a = 1.0 + nrm(ks[10], (N_A_LAYERS, D_MODEL), 0.02)
    w_in_a = nrm(ks[11], (N_A_LAYERS, D_MODEL, 4 * D_CONV), D_MODEL ** -0.5)
    conv_w_a = nrm(ks[12], (N_A_LAYERS, CONV_W, D_CONV), CONV_W ** -0.5)
    w_out_a = nrm(ks[13], (N_A_LAYERS, D_CONV, D_MODEL), 0.5 * D_CONV ** -0.5)
    norm_kv = 1.0 + nrm(ks[14], (D_MODEL,), 0.02)
    w_kv = nrm(ks[15], (D_MODEL, 2 * D_ATT + N_HEADS), D_MODEL ** -0.5)
    b_f = 2.0 + 4.0 * jax.random.uniform(ks[16], (N_HEADS,), f32)
    norm_b = 1.0 + nrm(ks[17], (N_B_LAYERS, D_MODEL), 0.02)
    w_in_b = nrm(ks[18], (N_B_LAYERS, D_MODEL, 2 * D_ATT), D_MODEL ** -0.5)
    w_out_b = nrm(ks[19], (N_B_LAYERS, D_ATT, D_MODEL), 0.5 * D_ATT ** -0.5)
    w_ple_proj = nrm(ks[20], (DEPTH, D_PLE, D_MODEL), 0.5 * D_PLE ** -0.5)
    w_ple_gate = nrm(ks[21], (DEPTH, D_MODEL, D_MODEL), D_MODEL ** -0.5)
    norm_f = 1.0 + nrm(ks[22], (D_MODEL,), 0.02)
    return {
        "x_prompt": x_prompt, "x_sample": x_sample, "p_prompt": p_prompt, "p_sample": p_sample,
        "state_conv": state_conv, "cache_k": cache_k, "cache_v": cache_v, "cache_logf": cache_logf,
        "page_table": page_table,
        "norm_a": norm_a, "w_in_a": w_in_a, "conv_w_a": conv_w_a, "w_out_a": w_out_a,
        "norm_kv": norm_kv, "w_kv": w_kv, "b_f": b_f,
        "norm_b": norm_b, "w_in_b": w_in_b, "w_out_b": w_out_b,
        "w_ple_proj": w_ple_proj, "w_ple_gate": w_ple_gate, "norm_f": norm_f,
    }


def reference(x_prompt, x_sample, p_prompt, p_sample, state_conv, cache_k, cache_v, cache_logf,
              page_table, norm_a, w_in_a, conv_w_a, w_out_a, norm_kv, w_kv, b_f,
              norm_b, w_in_b, w_out_b, w_ple_proj, w_ple_gate, norm_f):
    weights = (norm_a, w_in_a, conv_w_a, w_out_a, norm_kv, w_kv, b_f,
               norm_b, w_in_b, w_out_b, w_ple_proj, w_ple_gate, norm_f)

    zero_conv = jnp.zeros((N_A_LAYERS, x_prompt.shape[0], CONV_W - 1, D_CONV), x_prompt.dtype)

    def make_prompt_attend(k, v, logf):
        return lambda q: prompt_attention(q, k, v, logf)

    y_prompt, conv_prompt, k_prompt, v_prompt, logf_prompt = trunk(
        x_prompt, p_prompt, zero_conv, make_prompt_attend, *weights)

    db = x_sample.shape[0]
    k_past = cache_k[page_table].reshape(db, -1, N_HEADS, HEAD_DIM)
    v_past = cache_v[page_table].reshape(db, -1, N_HEADS, HEAD_DIM)
    logf_past = cache_logf[page_table].reshape(db, -1, N_HEADS).astype(jnp.float32)

    def make_sample_attend(k, v, logf):
        return lambda q: sample_attention(q, k, v, logf, k_past, v_past, logf_past)

    y_sample, conv_sample, k_sample, v_sample, logf_sample = trunk(
        x_sample, p_sample, state_conv, make_sample_attend, *weights)

    return (y_prompt, y_sample, conv_prompt, k_prompt, v_prompt, logf_prompt,
            conv_sample, k_sample, v_sample, logf_sample)
```

```python
import functools

import numpy as np
import jax
import jax.numpy as jnp
from jax import lax
from jax.experimental import pallas as pl
from jax.experimental.pallas import tpu as pltpu

F32 = jnp.float32
BF16 = jnp.bfloat16

N_HEADS = 16
HEAD_DIM = 64
D_ATT = N_HEADS * HEAD_DIM
CONV_W = 3
EPS = 1e-6
NEG_INF = -1e30

LANES = 128
SUBLANES = 8
HEAD_PAD = LANES
N_PARTS = 3
ONES_LANE = N_HEADS
VMEM_LIMIT_BYTES = 56 * 2**20

ROW_TILE = 512
CONV_ROW_TILE = 256
Q_TILE = 512
PAGES_PER_STEP = 4

_HIGHEST = lax.Precision.HIGHEST


def _params(n_grid_dims):
    return pltpu.CompilerParams(
        dimension_semantics=("arbitrary",) * n_grid_dims,
        vmem_limit_bytes=VMEM_LIMIT_BYTES)


def _resident(shape):
    zeros = (0,) * len(shape)
    return pl.BlockSpec(shape, lambda *_: zeros, pipeline_mode=pl.Buffered(1))


def _rows(tm, width):
    return pl.BlockSpec((tm, width), lambda i: (i, 0))


def _dot(a, b):
    return jnp.dot(a, b, preferred_element_type=F32)


def _rmsnorm(x, g):
    return x * lax.rsqrt(jnp.mean(x * x, axis=-1, keepdims=True) + EPS) * g


def _silu(z):
    return z * jax.nn.sigmoid(z)


def _log_sigmoid(x):
    return jnp.minimum(x, 0.0) - jnp.log1p(jnp.exp(-jnp.abs(x)))


def _ple_residual(x1, p_ref, wg_ref, wp_ref):
    gate = jax.nn.sigmoid(_dot(x1.astype(BF16), wg_ref[...]))
    proj = _dot(p_ref[...].astype(BF16), wp_ref[...])
    return x1 + gate * proj


def _seq_index(idx, seq_len):
    if seq_len & (seq_len - 1) == 0:
        shift = seq_len.bit_length() - 1
        return lax.shift_right_logical(idx, shift), idx & (seq_len - 1)
    return idx // seq_len, idx % seq_len


def _split_bf16(c):
    hi = c.astype(BF16)
    r1 = c - hi.astype(F32)
    mid = r1.astype(BF16)
    lo = (r1 - mid.astype(F32)).astype(BF16)
    return hi, mid, lo


def _bias_pieces(c):
    lane = lax.broadcasted_iota(jnp.int32, c.shape, 1)
    hi, mid, lo = _split_bf16(jnp.where(lane < N_HEADS, c, 0.0))
    hi = jnp.where(lane == ONES_LANE, 1.0, hi).astype(BF16)
    return jnp.concatenate([hi, mid, lo], axis=1)


def _conv_core(x_ref, p_ref, g_ref, win_ref, cw_ref, wout_ref, wg_ref, wp_ref, xo_ref, ubuf,
               fix_first_rows):
    tm, d = x_ref.shape
    x = x_ref[...]
    h = _rmsnorm(x, g_ref[...]).astype(BF16)
    proj = _dot(h, win_ref[...])
    b_gate, c_gate, xin, z = (proj[:, k * d:(k + 1) * d] for k in range(4))
    u = c_gate * xin
    ubuf[SUBLANES:SUBLANES + tm, :] = u
    u1 = ubuf[SUBLANES - 1:SUBLANES - 1 + tm, :]
    u2 = ubuf[SUBLANES - 2:SUBLANES - 2 + tm, :]
    u1, u2 = fix_first_rows(u1, u2)
    cw = cw_ref[...]
    conv = cw[0:1, :] * u2 + cw[1:2, :] * u1 + cw[2:3, :] * u
    y = b_gate * conv * _silu(z)
    x1 = x + _dot(y.astype(BF16), wout_ref[...])
    xo_ref[...] = _ple_residual(x1, p_ref, wg_ref, wp_ref)
    return u


def _conv_long_kernel(x_ref, p_ref, g_ref, win_ref, cw_ref, wout_ref, wg_ref, wp_ref,
                      xo_ref, tail_ref, ubuf, *, tiles_per_seq):
    tm, d = x_ref.shape
    first = (pl.program_id(0) % tiles_per_seq) == 0

    @pl.when(first)
    def _():
        ubuf[0:SUBLANES, :] = jnp.zeros((SUBLANES, d), F32)

    @pl.when(jnp.logical_not(first))
    def _():
        ubuf[0:SUBLANES, :] = ubuf[tm:tm + SUBLANES, :]

    _conv_core(x_ref, p_ref, g_ref, win_ref, cw_ref, wout_ref, wg_ref, wp_ref, xo_ref, ubuf,
               lambda u1, u2: (u1, u2))
    tail_ref[0] = ubuf[tm:tm + SUBLANES, :]


def _conv_short_kernel(x_ref, p_ref, s1_ref, s2_ref, g_ref, win_ref, cw_ref, wout_ref, wg_ref,
                       wp_ref, xo_ref, u_ref, ubuf, *, seq_len):
    tm, d = x_ref.shape
    ubuf[0:SUBLANES, :] = jnp.zeros((SUBLANES, d), F32)
    _, t = _seq_index(lax.broadcasted_iota(jnp.int32, (tm, d), 0), seq_len)

    def fix(u1, u2):
        return (jnp.where(t < 1, s1_ref[...], u1), jnp.where(t < 2, s2_ref[...], u2))

    u_ref[...] = _conv_core(x_ref, p_ref, g_ref, win_ref, cw_ref, wout_ref, wg_ref, wp_ref,
                            xo_ref, ubuf, fix)


def _conv_layer(x, p, state, w, seq_len):
    n, d = x.shape
    dp = p.shape[1]
    n_seq = n // seq_len
    weights = (w["g"], w["w_in"], w["conv_w"], w["w_out"], w["w_gate"], w["w_proj"])
    w_specs = [_resident(a.shape) for a in weights]
    if state is None:
        tm = min(CONV_ROW_TILE, seq_len)
        assert seq_len % tm == 0 and tm % SUBLANES == 0
        x_new, tails = pl.pallas_call(
            functools.partial(_conv_long_kernel, tiles_per_seq=seq_len // tm),
            grid=(n // tm,),
            in_specs=[_rows(tm, d), _rows(tm, dp)] + w_specs,
            out_specs=[_rows(tm, d), pl.BlockSpec((1, SUBLANES, d), lambda i: (i, 0, 0))],
            out_shape=[jax.ShapeDtypeStruct((n, d), F32),
                       jax.ShapeDtypeStruct((n // tm, SUBLANES, d), F32)],
            scratch_shapes=[pltpu.VMEM((tm + SUBLANES, d), F32)],
            compiler_params=_params(1),
            name="conv_layer_prompt",
        )(x, p, *weights)
        last = tails.reshape(n_seq, seq_len // tm, SUBLANES, d)[:, -1]
        return x_new, last[:, SUBLANES - (CONV_W - 1):, :]
    tm = n
    assert tm % seq_len == 0 and seq_len >= CONV_W - 1
    zeros = jnp.zeros((n_seq, seq_len, d), F32)
    s1 = zeros.at[:, 0].set(state[:, 1]).reshape(n, d)
    s2 = zeros.at[:, 0].set(state[:, 0]).at[:, 1].set(state[:, 1]).reshape(n, d)
    x_new, u = pl.pallas_call(
        functools.partial(_conv_short_kernel, seq_len=seq_len),
        grid=(n // tm,),
        in_specs=[_rows(tm, d), _rows(tm, dp), _rows(tm, d), _rows(tm, d)] + w_specs,
        out_specs=[_rows(tm, d), _rows(tm, d)],
        out_shape=[jax.ShapeDtypeStruct((n, d), F32), jax.ShapeDtypeStruct((n, d), F32)],
        scratch_shapes=[pltpu.VMEM((tm + SUBLANES, d), F32)],
        compiler_params=_params(1),
        name="conv_layer_sample",
    )(x, p, s1, s2, *weights)
    return x_new, u.reshape(n_seq, seq_len, d)[:, seq_len - (CONV_W - 1):, :]


def _cumsum_rows(vals, seq_len):
    tm = vals.shape[0]
    r = lax.broadcasted_iota(jnp.int32, (tm, tm), 0)
    c = lax.broadcasted_iota(jnp.int32, (tm, tm), 1)
    if seq_len < tm:
        lower = jnp.where(c <= r, _seq_index(c, seq_len)[0], -1)
        tri = jnp.where(lower == _seq_index(r, seq_len)[0], 1.0, 0.0)
    else:
        tri = jnp.where(c <= r, 1.0, 0.0)
    return jnp.dot(tri.astype(F32), vals, preferred_element_type=F32, precision=_HIGHEST)


def _kv_project(x_ref, g_ref, wa_ref, bf_ref, k_ref, v_ref, logf_ref):
    h = _rmsnorm(x_ref[...], g_ref[...]).astype(BF16)
    r = _dot(h, wa_ref[...])
    k = r[:, :D_ATT]
    v = r[:, D_ATT:2 * D_ATT]
    logf = _log_sigmoid(r[:, 2 * D_ATT:] + bf_ref[...])
    k_ref[...] = k
    v_ref[...] = v
    logf_ref[...] = logf[:, :N_HEADS]
    return h, v, logf


def _kv_long_kernel(x_ref, g_ref, wa_ref, bf_ref, wb_ref, k_ref, v_ref, logf_ref, c_ref,
                    ka_ref, vb_ref, carry, *, tiles_per_seq):
    tm = x_ref.shape[0]

    @pl.when((pl.program_id(0) % tiles_per_seq) == 0)
    def _():
        carry[...] = jnp.zeros(carry.shape, F32)

    h, v, logf = _kv_project(x_ref, g_ref, wa_ref, bf_ref, k_ref, v_ref, logf_ref)
    c = _cumsum_rows(logf, tm) + carry[0:1, :]
    carry[...] = jnp.broadcast_to(c[tm - 1:tm, :], carry.shape)
    c_ref[...] = c
    lhs = jnp.concatenate([h, _bias_pieces(c)], axis=1)
    ka_ref[...] = _dot(lhs, wb_ref[...]).astype(BF16)
    vb_ref[...] = v.astype(BF16)


def _kv_short_kernel(x_ref, g_ref, wa_ref, bf_ref, k_ref, v_ref, logf_ref, c_ref, *, seq_len):
    _, _, logf = _kv_project(x_ref, g_ref, wa_ref, bf_ref, k_ref, v_ref, logf_ref)
    c_ref[...] = _cumsum_rows(logf, seq_len)


def _shared_kv(x, w, seq_len, long_seq):
    n, d = x.shape
    row_out = [jax.ShapeDtypeStruct((n, D_ATT), F32), jax.ShapeDtypeStruct((n, D_ATT), F32),
               jax.ShapeDtypeStruct((n, N_HEADS), F32), jax.ShapeDtypeStruct((n, LANES), F32)]
    if long_seq:
        tm = min(ROW_TILE, seq_len)
        assert seq_len % tm == 0
        weights = (w["g_kv"], w["w_kvf"], w["b_f"], w["w_k_aug"])
        return pl.pallas_call(
            functools.partial(_kv_long_kernel, tiles_per_seq=seq_len // tm),
            grid=(n // tm,),
            in_specs=[_rows(tm, d)] + [_resident(a.shape) for a in weights],
            out_specs=[_rows(tm, D_ATT), _rows(tm, D_ATT), _rows(tm, N_HEADS), _rows(tm, LANES),
                       _rows(tm, N_HEADS * HEAD_PAD), _rows(tm, D_ATT)],
            out_shape=row_out + [jax.ShapeDtypeStruct((n, N_HEADS * HEAD_PAD), BF16),
                                 jax.ShapeDtypeStruct((n, D_ATT), BF16)],
            scratch_shapes=[pltpu.VMEM((SUBLANES, LANES), F32)],
            compiler_params=_params(1),
            name="shared_kv_prompt",
        )(x, *weights)
    tm = n
    assert tm % seq_len == 0
    weights = (w["g_kv"], w["w_kvf"], w["b_f"])
    return pl.pallas_call(
        functools.partial(_kv_short_kernel, seq_len=seq_len),
        grid=(n // tm,),
        in_specs=[_rows(tm, d)] + [_resident(a.shape) for a in weights],
        out_specs=[_rows(tm, D_ATT), _rows(tm, D_ATT), _rows(tm, N_HEADS), _rows(tm, LANES)],
        out_shape=row_out,
        compiler_params=_params(1),
        name="shared_kv_sample",
    )(x, *weights)


def _attn_in_prompt_kernel(x_ref, c_ref, g_ref, wq_ref, wz_ref, qa_ref, sz_ref):
    h = _rmsnorm(x_ref[...], g_ref[...]).astype(BF16)
    lhs = jnp.concatenate([h, _bias_pieces(c_ref[...])], axis=1)
    qa_ref[...] = _dot(lhs, wq_ref[...]).astype(BF16)
    sz_ref[...] = _silu(_dot(h, wz_ref[...]))


def _attn_in_sample_kernel(x_ref, g_ref, wq_ref, wz_ref, q_ref, sz_ref):
    h = _rmsnorm(x_ref[...], g_ref[...]).astype(BF16)
    q_ref[...] = _dot(h, wq_ref[...])
    sz_ref[...] = _silu(_dot(h, wz_ref[...]))


def _attn_in(x, c, w, long_seq):
    n, d = x.shape
    tm = min(ROW_TILE, n)
    if long_seq:
        weights = (w["g"], w["w_q_aug"], w["w_z"])
        return pl.pallas_call(
            _attn_in_prompt_kernel,
            grid=(n // tm,),
            in_specs=[_rows(tm, d), _rows(tm, LANES)] + [_resident(a.shape) for a in weights],
            out_specs=[_rows(tm, N_HEADS * HEAD_PAD), _rows(tm, D_ATT)],
            out_shape=[jax.ShapeDtypeStruct((n, N_HEADS * HEAD_PAD), BF16),
                       jax.ShapeDtypeStruct((n, D_ATT), F32)],
            compiler_params=_params(1),
            name="attn_in_prompt",
        )(x, c, *weights)
    weights = (w["g"], w["w_q"], w["w_z"])
    return pl.pallas_call(
        _attn_in_sample_kernel,
        grid=(n // tm,),
        in_specs=[_rows(tm, d)] + [_resident(a.shape) for a in weights],
        out_specs=[_rows(tm, D_ATT), _rows(tm, D_ATT)],
        out_shape=[jax.ShapeDtypeStruct((n, D_ATT), F32), jax.ShapeDtypeStruct((n, D_ATT), F32)],
        compiler_params=_params(1),
        name="attn_in_sample",
    )(x, *weights)


def _attn_out_kernel(x_ref, o_ref, sz_ref, p_ref, wout_ref, wg_ref, wp_ref, gf_ref, out_ref, *,
                     final):
    y = (o_ref[...] * sz_ref[...]).astype(BF16)
    x1 = x_ref[...] + _dot(y, wout_ref[...])
    x_new = _ple_residual(x1, p_ref, wg_ref, wp_ref)
    out_ref[...] = _rmsnorm(x_new, gf_ref[...]) if final else x_new


def _attn_out(x, o, sz, p, w, g_final, final):
    n, d = x.shape
    tm = min(ROW_TILE, n)
    weights = (w["w_out"], w["w_gate"], w["w_proj"], g_final)
    return pl.pallas_call(
        functools.partial(_attn_out_kernel, final=final),
        grid=(n // tm,),
        in_specs=[_rows(tm, d), _rows(tm, D_ATT), _rows(tm, D_ATT), _rows(tm, p.shape[1])]
        + [_resident(a.shape) for a in weights],
        out_specs=_rows(tm, d),
        out_shape=jax.ShapeDtypeStruct((n, d), F32),
        compiler_params=_params(1),
        name="attn_out",
    )(x, o, sz, p, *weights)


def _softmax_step(q, kt, vt, carry, mask):
    m, l, acc = carry
    s = lax.dot_general(q, kt, (((1,), (1,)), ((), ())), preferred_element_type=F32)
    if mask is not None:
        s = jnp.where(mask, s, NEG_INF)
    m_new = jnp.maximum(m, jnp.max(s, axis=-1, keepdims=True))
    alpha = jnp.exp(m - m_new)
    p = jnp.exp(s - m_new)
    l_new = alpha * l + jnp.sum(p, axis=-1, keepdims=True)
    acc_new = alpha * acc + _dot(p.astype(BF16), vt)
    return m_new, l_new, acc_new


def _flash_kernel(q_ref, k_ref, v_ref, o_ref, *, tile):
    i = pl.program_id(2)
    rows = lax.broadcasted_iota(jnp.int32, (tile, tile), 0)
    cols = lax.broadcasted_iota(jnp.int32, (tile, tile), 1)
    heads = []
    for hh in range(2):
        lo = hh * HEAD_PAD
        q = q_ref[0, :, lo:lo + HEAD_PAD]

        def kv_tile(j, lo=lo):
            off = pl.multiple_of(j * tile, tile)
            return k_ref[0, pl.ds(off, tile), lo:lo + HEAD_PAD], v_ref[0, pl.ds(off, tile), :]

        def body(j, carry, q=q, kv_tile=kv_tile):
            kt, vt = kv_tile(j)
            return _softmax_step(q, kt, vt, carry, None)

        init = (jnp.full((tile, 1), NEG_INF, F32), jnp.zeros((tile, 1), F32),
                jnp.zeros((tile, LANES), F32))
        carry = lax.fori_loop(0, i, body, init)
        kt, vt = kv_tile(i)
        _, l, acc = _softmax_step(q, kt, vt, carry, cols <= rows)
        heads.append(acc / l)
    lane = lax.broadcasted_iota(jnp.int32, (tile, LANES), 1)
    o_ref[0] = jnp.where(lane < HEAD_DIM, heads[0], heads[1])


def _prompt_attention(qa, ka, vb, n_seq, seq_len):
    tile = min(Q_TILE, seq_len)
    assert seq_len % tile == 0
    pair = 2 * HEAD_PAD
    qa3 = qa.reshape(n_seq, seq_len, N_HEADS * HEAD_PAD)
    ka3 = ka.reshape(n_seq, seq_len, N_HEADS * HEAD_PAD)
    vb3 = vb.reshape(n_seq, seq_len, D_ATT)
    o = pl.pallas_call(
        functools.partial(_flash_kernel, tile=tile),
        grid=(n_seq, N_HEADS // 2, seq_len // tile),
        in_specs=[pl.BlockSpec((1, tile, pair), lambda b, hp, i: (b, i, hp)),
                  pl.BlockSpec((1, seq_len, pair), lambda b, hp, i: (b, 0, hp)),
                  pl.BlockSpec((1, seq_len, LANES), lambda b, hp, i: (b, 0, hp))],
        out_specs=pl.BlockSpec((1, tile, LANES), lambda b, hp, i: (b, i, hp)),
        out_shape=jax.ShapeDtypeStruct((n_seq, seq_len, D_ATT), F32),
        compiler_params=_params(3),
        name="prompt_attention",
    )(qa3, ka3, vb3)
    return o.reshape(n_seq * seq_len, D_ATT)


def _online_update(s, v, m_s, l_s, acc_s):
    m_prev = m_s[...]
    m_new = jnp.maximum(m_prev, jnp.max(s, axis=-1, keepdims=True))
    alpha = jnp.exp(m_prev - m_new)
    p = jnp.exp(s - m_new)
    l_s[...] = alpha * l_s[...] + jnp.sum(p, axis=-1, keepdims=True)
    pv = jnp.einsum("htk,hkd->htd", p.astype(BF16), v, preferred_element_type=F32)
    acc_s[...] = alpha * acc_s[...] + pv
    m_s[...] = m_new


def _add_key_bias(s, bias):
    return jnp.stack([s[h] + bias[h:h + 1, :] for h in range(N_HEADS)])


def _sample_attn_kernel(pt_ref, q_ref, cnq_ref, kn_ref, vn_ref, cnk_ref, *refs, ppc):
    del pt_ref
    k_refs, v_refs, f_refs = refs[:ppc], refs[ppc:2 * ppc], refs[2 * ppc:3 * ppc]
    o_ref, m_s, l_s, acc_s, tot_s = refs[3 * ppc:]
    step = pl.program_id(1)

    @pl.when(step == 0)
    def _():
        m_s[...] = jnp.full(m_s.shape, NEG_INF, F32)
        l_s[...] = jnp.zeros(l_s.shape, F32)
        acc_s[...] = jnp.zeros(acc_s.shape, F32)
        tot_s[...] = jnp.zeros(tot_s.shape, F32)

    q = q_ref[0].astype(BF16)
    to_head_major = lambda r: pltpu.einshape("khd->hkd", r[0])
    kc = jnp.concatenate([to_head_major(r) for r in k_refs], axis=1).astype(BF16)
    vc = jnp.concatenate([to_head_major(r) for r in v_refs], axis=1).astype(BF16)
    n_keys = kc.shape[1]

    logf = jnp.concatenate([r[0] for r in f_refs], axis=0)
    eye = jnp.where(lax.broadcasted_iota(jnp.int32, (N_HEADS, N_HEADS), 0)
                    == lax.broadcasted_iota(jnp.int32, (N_HEADS, N_HEADS), 1), 1.0, 0.0)
    logf_t = lax.dot_general(eye.astype(F32), logf, (((1,), (1,)), ((), ())),
                             preferred_element_type=F32, precision=_HIGHEST)
    later = jnp.where(lax.broadcasted_iota(jnp.int32, (n_keys, n_keys), 0)
                      > lax.broadcasted_iota(jnp.int32, (n_keys, n_keys), 1), 1.0, 0.0)
    rest = jnp.dot(logf_t, later.astype(F32), preferred_element_type=F32,
                   precision=_HIGHEST) + tot_s[...]
    tot_s[...] = tot_s[...] + jnp.sum(logf_t, axis=1, keepdims=True)

    s = jnp.einsum("htd,hkd->htk", q, kc, preferred_element_type=F32)
    s = _add_key_bias(s, rest) + cnq_ref[0]
    _online_update(s, vc, m_s, l_s, acc_s)

    @pl.when(step == pl.num_programs(1) - 1)
    def _():
        kn = kn_ref[0].astype(BF16)
        s2 = jnp.einsum("htd,hkd->htk", q, kn, preferred_element_type=F32)
        s2 = _add_key_bias(s2, -cnk_ref[0]) + cnq_ref[0]
        t = lax.broadcasted_iota(jnp.int32, s2.shape, 1)
        sp = lax.broadcasted_iota(jnp.int32, s2.shape, 2)
        s2 = jnp.where(sp <= t, s2, NEG_INF)
        _online_update(s2, vn_ref[0].astype(BF16), m_s, l_s, acc_s)
        o_ref[0] = acc_s[...] / l_s[...]


def _sample_attention(q, k_new, v_new, c_new, cache_k, cache_v, cache_logf, page_table, seq_len):
    n = q.shape[0]
    n_seq = n // seq_len
    n_pages = page_table.shape[1]
    page = cache_k.shape[1]
    ppc = PAGES_PER_STEP
    assert n_pages % ppc == 0 and seq_len <= LANES
    n_chunks = n_pages // ppc

    def heads_first(a):
        return a.reshape(n_seq, seq_len, N_HEADS, HEAD_DIM).transpose(0, 2, 1, 3)

    pad_keys = lambda a: jnp.pad(a, ((0, 0), (0, 0), (0, LANES - seq_len), (0, 0)))
    cn = c_new[:, :N_HEADS].reshape(n_seq, seq_len, N_HEADS).transpose(0, 2, 1)
    cnq = cn[..., None]
    cnk = jnp.pad(cn, ((0, 0), (0, 0), (0, LANES - seq_len)))

    def page_map(j):
        return lambda b, c, pt: (pt[b, (n_chunks - 1 - c) * ppc + j], 0, 0, 0)

    def logf_map(j):
        return lambda b, c, pt: (pt[b, (n_chunks - 1 - c) * ppc + j], 0, 0)

    per_seq4 = lambda s2, s3: pl.BlockSpec((1, N_HEADS, s2, s3), lambda b, c, pt: (b, 0, 0, 0))
    kv_page = lambda j: pl.BlockSpec((1, page, N_HEADS, HEAD_DIM), page_map(j))
    o = pl.pallas_call(
        functools.partial(_sample_attn_kernel, ppc=ppc),
        grid_spec=pltpu.PrefetchScalarGridSpec(
            num_scalar_prefetch=1,
            grid=(n_seq, n_chunks),
            in_specs=[per_seq4(seq_len, HEAD_DIM), per_seq4(seq_len, 1),
                      per_seq4(LANES, HEAD_DIM), per_seq4(LANES, HEAD_DIM),
                      pl.BlockSpec((1, N_HEADS, LANES), lambda b, c, pt: (b, 0, 0))]
            + [kv_page(j) for j in range(ppc)] + [kv_page(j) for j in range(ppc)]
            + [pl.BlockSpec((1, page, N_HEADS), logf_map(j)) for j in range(ppc)],
            out_specs=per_seq4(seq_len, HEAD_DIM),
            scratch_shapes=[pltpu.VMEM((N_HEADS, seq_len, 1), F32),
                            pltpu.VMEM((N_HEADS, seq_len, 1), F32),
                            pltpu.VMEM((N_HEADS, seq_len, HEAD_DIM), F32),
                            pltpu.VMEM((N_HEADS, 1), F32)]),
        out_shape=jax.ShapeDtypeStruct((n_seq, N_HEADS, seq_len, HEAD_DIM), F32),
        compiler_params=_params(2),
        name="sample_attention",
    )(page_table, heads_first(q), cnq, pad_keys(heads_first(k_new)), pad_keys(heads_first(v_new)),
      cnk, *([cache_k] * ppc), *([cache_v] * ppc), *([cache_logf] * ppc))
    return o.transpose(0, 2, 1, 3).reshape(n, D_ATT)


def _pad_heads(w):
    d = w.shape[0]
    w = w.reshape(d, N_HEADS, HEAD_DIM)
    return jnp.pad(w, ((0, 0), (0, 0), (0, HEAD_PAD - HEAD_DIM))).reshape(d, N_HEADS * HEAD_PAD)


def _placement_rows(piece_col0, piece_sign, ones_col0):
    rows = np.zeros((N_PARTS * LANES, N_HEADS * HEAD_PAD), np.float32)
    for h in range(N_HEADS):
        for part in range(N_PARTS):
            rows[part * LANES + h, h * HEAD_PAD + piece_col0 + part] = piece_sign
            rows[ONES_LANE, h * HEAD_PAD + ones_col0 + part] = 1.0
    return jnp.asarray(rows, BF16)


def _prepare_weights(norm_a, w_in_a, conv_w_a, w_out_a, norm_kv, w_kv, b_f, norm_b, w_in_b, w_out_b,
                     w_ple_proj, w_ple_gate, norm_f):
    n_a, n_b = w_in_a.shape[0], w_in_b.shape[0]
    scale = HEAD_DIM ** -0.5
    k_rows = _placement_rows(HEAD_DIM, -1.0, HEAD_DIM + N_PARTS)
    q_rows = _placement_rows(HEAD_DIM + N_PARTS, 1.0, HEAD_DIM)
    ple = lambda i: dict(w_gate=w_ple_gate[i].astype(BF16), w_proj=w_ple_proj[i].astype(BF16))
    conv = [dict(g=norm_a[i][None, :], w_in=w_in_a[i].astype(BF16), conv_w=conv_w_a[i],
                 w_out=w_out_a[i].astype(BF16), **ple(i)) for i in range(n_a)]
    w_f = jnp.pad(w_kv[:, 2 * D_ATT:], ((0, 0), (0, LANES - N_HEADS)))
    kv = dict(g_kv=norm_kv[None, :],
              w_kvf=jnp.concatenate([w_kv[:, :2 * D_ATT], w_f], axis=1).astype(BF16),
              b_f=jnp.pad(b_f, (0, LANES - N_HEADS))[None, :],
              w_k_aug=jnp.concatenate([_pad_heads(w_kv[:, :D_ATT]).astype(BF16), k_rows], axis=0))
    attn = []
    for j in range(n_b):
        w_q = w_in_b[j][:, :D_ATT] * scale
        attn.append(dict(g=norm_b[j][None, :], w_q=w_q.astype(BF16),
                         w_q_aug=jnp.concatenate([_pad_heads(w_q).astype(BF16), q_rows], axis=0),
                         w_z=w_in_b[j][:, D_ATT:].astype(BF16),
                         w_out=w_out_b[j].astype(BF16), **ple(n_a + j)))
    return conv, kv, attn, norm_f[None, :]


def _trunk(x3, p4, conv_state, weights, attend):
    conv_w, kv_w, attn_w, g_final = weights
    n_seq, seq_len, d = x3.shape
    n = n_seq * seq_len
    long_seq = conv_state is None
    x = x3.reshape(n, d)
    p = p4.reshape(p4.shape[0], n, p4.shape[-1])
    new_conv = []
    for i, w in enumerate(conv_w):
        x, st = _conv_layer(x, p[i], None if long_seq else conv_state[i], w, seq_len)
        new_conv.append(st)
    kv = _shared_kv(x, kv_w, seq_len, long_seq)
    k, v, logf, c = kv[:4]
    for j, w in enumerate(attn_w):
        q, sz = _attn_in(x, c, w, long_seq)
        o = attend(q, kv)
        x = _attn_out(x, o, sz, p[len(conv_w) + j], w, g_final, final=(j == len(attn_w) - 1))
    return (x.reshape(n_seq, seq_len, d), jnp.stack(new_conv),
            k.reshape(n_seq, seq_len, N_HEADS, HEAD_DIM), v.reshape(n_seq, seq_len, N_HEADS, HEAD_DIM),
            logf.reshape(n_seq, seq_len, N_HEADS))


def kernel(x_prompt, x_sample, p_prompt, p_sample, state_conv, cache_k, cache_v, cache_logf, page_table, norm_a, w_in_a, conv_w_a, w_out_a, norm_kv, w_kv, b_f, norm_b, w_in_b, w_out_b, w_ple_proj, w_ple_gate, norm_f):
    assert conv_w_a.shape[1] == CONV_W and w_kv.shape[1] == 2 * D_ATT + N_HEADS
    weights = _prepare_weights(norm_a, w_in_a, conv_w_a, w_out_a, norm_kv, w_kv, b_f, norm_b, w_in_b,
                               w_out_b, w_ple_proj, w_ple_gate, norm_f)
    n_prompt, prompt_len = x_prompt.shape[:2]
    sample_len = x_sample.shape[1]

    def prompt_attend(qa, kv):
        return _prompt_attention(qa, kv[4], kv[5], n_prompt, prompt_len)

    def sample_attend(q, kv):
        return _sample_attention(q, kv[0], kv[1], kv[3], cache_k, cache_v, cache_logf, page_table,
                                 sample_len)

    y_p, conv_p, k_p, v_p, logf_p = _trunk(x_prompt, p_prompt, None, weights, prompt_attend)
    y_s, conv_s, k_s, v_s, logf_s = _trunk(x_sample, p_sample, state_conv, weights, sample_attend)
    return (y_p, y_s, conv_p, k_p, v_p, logf_p, conv_s, k_s, v_s, logf_s)
```

```python
import functools

import numpy as np
import jax
import jax.numpy as jnp
from jax import lax
from jax.experimental import pallas as pl
from jax.experimental.pallas import tpu as pltpu

F32 = jnp.float32
BF16 = jnp.bfloat16

N_HEADS = 16
HEAD_DIM = 64
D_ATT = N_HEADS * HEAD_DIM
CONV_W = 3
EPS = 1e-6
NEG_INF = -1e30

LANES = 128
SUBLANES = 8
HEAD_PAD = LANES
N_PARTS = 3
ONES_LANE = N_HEADS
VMEM_LIMIT_BYTES = 56 * 2**20

ROW_TILE = 512
CONV_ROW_TILE = 256
Q_TILE = 512
PAGES_PER_STEP = 4

_HIGHEST = lax.Precision.HIGHEST


def _params(n_grid_dims):
    return pltpu.CompilerParams(
        dimension_semantics=("arbitrary",) * n_grid_dims,
        vmem_limit_bytes=VMEM_LIMIT_BYTES)


def _resident(shape):
    zeros = (0,) * len(shape)
    return pl.BlockSpec(shape, lambda *_: zeros, pipeline_mode=pl.Buffered(1))


def _rows(tm, width):
    return pl.BlockSpec((tm, width), lambda i: (i, 0))


def _dot(a, b):
    return jnp.dot(a, b, preferred_element_type=F32)


def _rmsnorm(x, g):
    return x * lax.rsqrt(jnp.mean(x * x, axis=-1, keepdims=True) + EPS) * g


def _silu(z):
    return z * jax.nn.sigmoid(z)


def _log_sigmoid(x):
    return jnp.minimum(x, 0.0) - jnp.log1p(jnp.exp(-jnp.abs(x)))


def _ple_residual(x1, p_ref, wg_ref, wp_ref):
    gate = jax.nn.sigmoid(_dot(x1.astype(BF16), wg_ref[...]))
    proj = _dot(p_ref[...].astype(BF16), wp_ref[...])
    return x1 + gate * proj


def _seq_index(idx, seq_len):
    if seq_len & (seq_len - 1) == 0:
        shift = seq_len.bit_length() - 1
        return lax.shift_right_logical(idx, shift), idx & (seq_len - 1)
    return idx // seq_len, idx % seq_len


def _split_bf16(c):
    hi = c.astype(BF16)
    r1 = c - hi.astype(F32)
    mid = r1.astype(BF16)
    lo = (r1 - mid.astype(F32)).astype(BF16)
    return hi, mid, lo


def _bias_pieces(c):
    lane = lax.broadcasted_iota(jnp.int32, c.shape, 1)
    hi, mid, lo = _split_bf16(jnp.where(lane < N_HEADS, c, 0.0))
    hi = jnp.where(lane == ONES_LANE, 1.0, hi).astype(BF16)
    return jnp.concatenate([hi, mid, lo], axis=1)


def _conv_core(x_ref, p_ref, g_ref, win_ref, cw_ref, wout_ref, wg_ref, wp_ref, xo_ref, ubuf,
               fix_first_rows):
    tm, d = x_ref.shape
    x = x_ref[...]
    h = _rmsnorm(x, g_ref[...]).astype(BF16)
    proj = _dot(h, win_ref[...])
    b_gate, c_gate, xin, z = (proj[:, k * d:(k + 1) * d] for k in range(4))
    u = c_gate * xin
    ubuf[SUBLANES:SUBLANES + tm, :] = u
    u1 = ubuf[SUBLANES - 1:SUBLANES - 1 + tm, :]
    u2 = ubuf[SUBLANES - 2:SUBLANES - 2 + tm, :]
    u1, u2 = fix_first_rows(u1, u2)
    cw = cw_ref[...]
    conv = cw[0:1, :] * u2 + cw[1:2, :] * u1 + cw[2:3, :] * u
    y = b_gate * conv * _silu(z)
    x1 = x + _dot(y.astype(BF16), wout_ref[...])
    xo_ref[...] = _ple_residual(x1, p_ref, wg_ref, wp_ref)
    return u


def _conv_long_kernel(x_ref, p_ref, g_ref, win_ref, cw_ref, wout_ref, wg_ref, wp_ref,
                      xo_ref, tail_ref, ubuf, *, tiles_per_seq):
    tm, d = x_ref.shape
    first = (pl.program_id(0) % tiles_per_seq) == 0

    @pl.when(first)
    def _():
        ubuf[0:SUBLANES, :] = jnp.zeros((SUBLANES, d), F32)

    @pl.when(jnp.logical_not(first))
    def _():
        ubuf[0:SUBLANES, :] = ubuf[tm:tm + SUBLANES, :]

    _conv_core(x_ref, p_ref, g_ref, win_ref, cw_ref, wout_ref, wg_ref, wp_ref, xo_ref, ubuf,
               lambda u1, u2: (u1, u2))
    tail_ref[0] = ubuf[tm:tm + SUBLANES, :]


def _conv_short_kernel(x_ref, p_ref, s1_ref, s2_ref, g_ref, win_ref, cw_ref, wout_ref, wg_ref,
                       wp_ref, xo_ref, u_ref, ubuf, *, seq_len):
    tm, d = x_ref.shape
    ubuf[0:SUBLANES, :] = jnp.zeros((SUBLANES, d), F32)
    _, t = _seq_index(lax.broadcasted_iota(jnp.int32, (tm, d), 0), seq_len)

    def fix(u1, u2):
        return (jnp.where(t < 1, s1_ref[...], u1), jnp.where(t < 2, s2_ref[...], u2))

    u_ref[...] = _conv_core(x_ref, p_ref, g_ref, win_ref, cw_ref, wout_ref, wg_ref, wp_ref,
                            xo_ref, ubuf, fix)


def _conv_layer(x, p, state, w, seq_len):
    n, d = x.shape
    dp = p.shape[1]
    n_seq = n // seq_len
    weights = (w["g"], w["w_in"], w["conv_w"], w["w_out"], w["w_gate"], w["w_proj"])
    w_specs = [_resident(a.shape) for a in weights]
    if state is None:
        tm = min(CONV_ROW_TILE, seq_len)
        assert seq_len % tm == 0 and tm % SUBLANES == 0
        x_new, tails = pl.pallas_call(
            functools.partial(_conv_long_kernel, tiles_per_seq=seq_len // tm),
            grid=(n // tm,),
            in_specs=[_rows(tm, d), _rows(tm, dp)] + w_specs,
            out_specs=[_rows(tm, d), pl.BlockSpec((1, SUBLANES, d), lambda i: (i, 0, 0))],
            out_shape=[jax.ShapeDtypeStruct((n, d), F32),
                       jax.ShapeDtypeStruct((n // tm, SUBLANES, d), F32)],
            scratch_shapes=[pltpu.VMEM((tm + SUBLANES, d), F32)],
            compiler_params=_params(1),
            name="conv_layer_prompt",
        )(x, p, *weights)
        last = tails.reshape(n_seq, seq_len // tm, SUBLANES, d)[:, -1]
        return x_new, last[:, SUBLANES - (CONV_W - 1):, :]
    tm = n
    assert tm % seq_len == 0 and seq_len >= CONV_W - 1
    zeros = jnp.zeros((n_seq, seq_len, d), F32)
    s1 = zeros.at[:, 0].set(state[:, 1]).reshape(n, d)
    s2 = zeros.at[:, 0].set(state[:, 0]).at[:, 1].set(state[:, 1]).reshape(n, d)
    x_new, u = pl.pallas_call(
        functools.partial(_conv_short_kernel, seq_len=seq_len),
        grid=(n // tm,),
        in_specs=[_rows(tm, d), _rows(tm, dp), _rows(tm, d), _rows(tm, d)] + w_specs,
        out_specs=[_rows(tm, d), _rows(tm, d)],
        out_shape=[jax.ShapeDtypeStruct((n, d), F32), jax.ShapeDtypeStruct((n, d), F32)],
        scratch_shapes=[pltpu.VMEM((tm + SUBLANES, d), F32)],
        compiler_params=_params(1),
        name="conv_layer_sample",
    )(x, p, s1, s2, *weights)
    return x_new, u.reshape(n_seq, seq_len, d)[:, seq_len - (CONV_W - 1):, :]


def _cumsum_rows(vals, seq_len):
    tm = vals.shape[0]
    r = lax.broadcasted_iota(jnp.int32, (tm, tm), 0)
    c = lax.broadcasted_iota(jnp.int32, (tm, tm), 1)
    if seq_len < tm:
        lower = jnp.where(c <= r, _seq_index(c, seq_len)[0], -1)
        tri = jnp.where(lower == _seq_index(r, seq_len)[0], 1.0, 0.0)
    else:
        tri = jnp.where(c <= r, 1.0, 0.0)
    return jnp.dot(tri.astype(F32), vals, preferred_element_type=F32, precision=_HIGHEST)


def _dot_nt(a, b):
    return lax.dot_general(a, b, (((1,), (1,)), ((), ())), preferred_element_type=F32)


def _kv_long_kernel(x_ref, g_ref, wvf_ref, bf_ref, wkt_ref, wvt_ref, wft_ref, bfc_ref,
                    kt_ref, vt_ref, logft_ref, c_ref, ka_ref, vb_ref, carry, *, tiles_per_seq):
    tm = x_ref.shape[0]

    @pl.when((pl.program_id(0) % tiles_per_seq) == 0)
    def _():
        carry[...] = jnp.zeros(carry.shape, F32)

    h = _rmsnorm(x_ref[...], g_ref[...]).astype(BF16)
    r = _dot(h, wvf_ref[...])
    vb_ref[...] = r[:, :D_ATT].astype(BF16)
    logf = _log_sigmoid(r[:, D_ATT:] + bf_ref[...])
    c = _cumsum_rows(logf, tm) + carry[0:1, :]
    carry[...] = jnp.broadcast_to(c[tm - 1:tm, :], carry.shape)
    c_ref[...] = c
    lhs = jnp.concatenate([h, _bias_pieces(c)], axis=1)
    kt_aug = _dot_nt(wkt_ref[...], lhs)
    ka_ref[0] = kt_aug.astype(BF16)
    for hd in range(N_HEADS):
        kt_ref[0, hd * HEAD_DIM:(hd + 1) * HEAD_DIM, :] = (
            kt_aug[hd * HEAD_PAD:hd * HEAD_PAD + HEAD_DIM, :])
    vt_ref[0] = _dot_nt(wvt_ref[...], h)
    logft_ref[0] = _log_sigmoid(_dot_nt(wft_ref[...], h) + bfc_ref[...])


def _kv_short_kernel(x_ref, g_ref, wa_ref, bf_ref, k_ref, v_ref, logf_ref, c_ref, *, seq_len):
    h = _rmsnorm(x_ref[...], g_ref[...]).astype(BF16)
    r = _dot(h, wa_ref[...])
    k_ref[...] = r[:, :D_ATT]
    v_ref[...] = r[:, D_ATT:2 * D_ATT]
    logf = _log_sigmoid(r[:, 2 * D_ATT:] + bf_ref[...])
    logf_ref[...] = logf[:, :N_HEADS]
    c_ref[...] = _cumsum_rows(logf, seq_len)


def _shared_kv(x, w, n_seq, seq_len, long_seq):
    n, d = x.shape
    if long_seq:
        tm = min(ROW_TILE, seq_len)
        assert seq_len % tm == 0
        tps = seq_len // tm
        weights = (w["g_kv"], w["w_vf"], w["b_f"], w["w_kt_aug"], w["w_vt"], w["w_ft"], w["b_f_col"])
        by_seq = lambda rows: pl.BlockSpec((1, rows, tm), lambda i: (i // tps, 0, i % tps))
        kt, vt, logft, c, ka, vb = pl.pallas_call(
            functools.partial(_kv_long_kernel, tiles_per_seq=tps),
            grid=(n // tm,),
            in_specs=[_rows(tm, d)] + [_resident(a.shape) for a in weights],
            out_specs=[by_seq(D_ATT), by_seq(D_ATT), by_seq(N_HEADS), _rows(tm, LANES),
                       by_seq(N_HEADS * HEAD_PAD), _rows(tm, D_ATT)],
            out_shape=[jax.ShapeDtypeStruct((n_seq, D_ATT, seq_len), F32),
                       jax.ShapeDtypeStruct((n_seq, D_ATT, seq_len), F32),
                       jax.ShapeDtypeStruct((n_seq, N_HEADS, seq_len), F32),
                       jax.ShapeDtypeStruct((n, LANES), F32),
                       jax.ShapeDtypeStruct((n_seq, N_HEADS * HEAD_PAD, seq_len), BF16),
                       jax.ShapeDtypeStruct((n, D_ATT), BF16)],
            scratch_shapes=[pltpu.VMEM((SUBLANES, LANES), F32)],
            compiler_params=_params(1),
            name="shared_kv_prompt",
        )(x, *weights)
        to_std = lambda a: a.reshape(n_seq, N_HEADS, HEAD_DIM, seq_len).transpose(0, 3, 1, 2)
        return dict(k=to_std(kt), v=to_std(vt), logf=logft.transpose(0, 2, 1), c=c, ka=ka, vb=vb)
    tm = n
    assert tm % seq_len == 0
    weights = (w["g_kv"], w["w_kvf"], w["b_f"])
    k, v, logf, c = pl.pallas_call(
        functools.partial(_kv_short_kernel, seq_len=seq_len),
        grid=(n // tm,),
        in_specs=[_rows(tm, d)] + [_resident(a.shape) for a in weights],
        out_specs=[_rows(tm, D_ATT), _rows(tm, D_ATT), _rows(tm, N_HEADS), _rows(tm, LANES)],
        out_shape=[jax.ShapeDtypeStruct((n, D_ATT), F32), jax.ShapeDtypeStruct((n, D_ATT), F32),
                   jax.ShapeDtypeStruct((n, N_HEADS), F32), jax.ShapeDtypeStruct((n, LANES), F32)],
        compiler_params=_params(1),
        name="shared_kv_sample",
    )(x, *weights)
    std = lambda a: a.reshape(n_seq, seq_len, N_HEADS, HEAD_DIM)
    return dict(k=std(k), v=std(v), logf=logf.reshape(n_seq, seq_len, N_HEADS), c=c, k2=k, v2=v)


def _attn_in_prompt_kernel(x_ref, c_ref, g_ref, wq_ref, wz_ref, qa_ref, sz_ref):
    h = _rmsnorm(x_ref[...], g_ref[...]).astype(BF16)
    lhs = jnp.concatenate([h, _bias_pieces(c_ref[...])], axis=1)
    qa_ref[...] = _dot(lhs, wq_ref[...]).astype(BF16)
    sz_ref[...] = _silu(_dot(h, wz_ref[...]))


def _attn_in_sample_kernel(x_ref, g_ref, wq_ref, wz_ref, q_ref, sz_ref):
    h = _rmsnorm(x_ref[...], g_ref[...]).astype(BF16)
    q_ref[...] = _dot(h, wq_ref[...])
    sz_ref[...] = _silu(_dot(h, wz_ref[...]))


def _attn_in(x, c, w, long_seq):
    n, d = x.shape
    tm = min(ROW_TILE, n)
    if long_seq:
        weights = (w["g"], w["w_q_aug"], w["w_z"])
        return pl.pallas_call(
            _attn_in_prompt_kernel,
            grid=(n // tm,),
            in_specs=[_rows(tm, d), _rows(tm, LANES)] + [_resident(a.shape) for a in weights],
            out_specs=[_rows(tm, N_HEADS * HEAD_PAD), _rows(tm, D_ATT)],
            out_shape=[jax.ShapeDtypeStruct((n, N_HEADS * HEAD_PAD), BF16),
                       jax.ShapeDtypeStruct((n, D_ATT), F32)],
            compiler_params=_params(1),
            name="attn_in_prompt",
        )(x, c, *weights)
    weights = (w["g"], w["w_q"], w["w_z"])
    return pl.pallas_call(
        _attn_in_sample_kernel,
        grid=(n // tm,),
        in_specs=[_rows(tm, d)] + [_resident(a.shape) for a in weights],
        out_specs=[_rows(tm, D_ATT), _rows(tm, D_ATT)],
        out_shape=[jax.ShapeDtypeStruct((n, D_ATT), F32), jax.ShapeDtypeStruct((n, D_ATT), F32)],
        compiler_params=_params(1),
        name="attn_in_sample",
    )(x, *weights)


def _attn_out_kernel(x_ref, o_ref, sz_ref, p_ref, wout_ref, wg_ref, wp_ref, gf_ref, out_ref, *,
                     final):
    y = (o_ref[...] * sz_ref[...]).astype(BF16)
    x1 = x_ref[...] + _dot(y, wout_ref[...])
    x_new = _ple_residual(x1, p_ref, wg_ref, wp_ref)
    out_ref[...] = _rmsnorm(x_new, gf_ref[...]) if final else x_new


def _attn_out(x, o, sz, p, w, g_final, final):
    n, d = x.shape
    tm = min(ROW_TILE, n)
    weights = (w["w_out"], w["w_gate"], w["w_proj"], g_final)
    return pl.pallas_call(
        functools.partial(_attn_out_kernel, final=final),
        grid=(n // tm,),
        in_specs=[_rows(tm, d), _rows(tm, D_ATT), _rows(tm, D_ATT), _rows(tm, p.shape[1])]
        + [_resident(a.shape) for a in weights],
        out_specs=_rows(tm, d),
        out_shape=jax.ShapeDtypeStruct((n, d), F32),
        compiler_params=_params(1),
        name="attn_out",
    )(x, o, sz, p, *weights)


def _softmax_step(q, kt, vt, carry, mask):
    m, l, acc = carry
    s = _dot(q, kt)
    if mask is not None:
        s = jnp.where(mask, s, NEG_INF)
    m_new = jnp.maximum(m, jnp.max(s, axis=-1, keepdims=True))
    alpha = jnp.exp(m - m_new)
    p = jnp.exp(s - m_new)
    l_new = alpha * l + jnp.sum(p, axis=-1, keepdims=True)
    acc_new = alpha * acc + _dot(p.astype(BF16), vt)
    return m_new, l_new, acc_new


def _flash_kernel(q_ref, k_ref, v_ref, o_ref, *, tile):
    i = pl.program_id(2)
    rows = lax.broadcasted_iota(jnp.int32, (tile, tile), 0)
    cols = lax.broadcasted_iota(jnp.int32, (tile, tile), 1)
    qs = [q_ref[0, :, hh * HEAD_PAD:(hh + 1) * HEAD_PAD] for hh in range(2)]

    def step(j, carry, mask):
        off = pl.multiple_of(j * tile, tile)
        vt = v_ref[0, pl.ds(off, tile), :]
        return tuple(
            _softmax_step(qs[hh], k_ref[0, hh * HEAD_PAD:(hh + 1) * HEAD_PAD, pl.ds(off, tile)], vt,
                          carry[hh], mask) for hh in range(2))

    init = (jnp.full((tile, 1), NEG_INF, F32), jnp.zeros((tile, 1), F32), jnp.zeros((tile, LANES), F32))
    carry = lax.fori_loop(0, i, lambda j, c: step(j, c, None), (init, init))
    (_, l0, acc0), (_, l1, acc1) = step(i, carry, cols <= rows)
    lane = lax.broadcasted_iota(jnp.int32, (tile, LANES), 1)
    o_ref[0] = jnp.where(lane < HEAD_DIM, acc0 / l0, acc1 / l1)


def _prompt_attention(qa, ka, vb, n_seq, seq_len):
    tile = min(Q_TILE, seq_len)
    assert seq_len % tile == 0
    pair = 2 * HEAD_PAD
    qa3 = qa.reshape(n_seq, seq_len, N_HEADS * HEAD_PAD)
    vb3 = vb.reshape(n_seq, seq_len, D_ATT)
    o = pl.pallas_call(
        functools.partial(_flash_kernel, tile=tile),
        grid=(n_seq, N_HEADS // 2, seq_len // tile),
        in_specs=[pl.BlockSpec((1, tile, pair), lambda b, hp, i: (b, i, hp)),
                  pl.BlockSpec((1, pair, seq_len), lambda b, hp, i: (b, hp, 0)),
                  pl.BlockSpec((1, seq_len, LANES), lambda b, hp, i: (b, 0, hp))],
        out_specs=pl.BlockSpec((1, tile, LANES), lambda b, hp, i: (b, i, hp)),
        out_shape=jax.ShapeDtypeStruct((n_seq, seq_len, D_ATT), F32),
        compiler_params=_params(3),
        name="prompt_attention",
    )(qa3, ka, vb3)
    return o.reshape(n_seq * seq_len, D_ATT)


def _online_update(s, vt, m_s, l_s, acc_s):
    m_prev = m_s[...]
    m_new = jnp.maximum(m_prev, jnp.max(s, axis=-1, keepdims=True))
    alpha = jnp.exp(m_prev - m_new)
    p = jnp.exp(s - m_new)
    l_s[...] = alpha * l_s[...] + jnp.sum(p, axis=-1, keepdims=True)
    pv = jnp.einsum("htk,hdk->htd", p.astype(BF16), vt, preferred_element_type=F32)
    acc_s[...] = alpha * acc_s[...] + pv
    m_s[...] = m_new


def _add_key_bias(s, bias):
    return jnp.stack([s[h] + bias[h:h + 1, :] for h in range(N_HEADS)])


def _sample_attn_kernel(pt_ref, q_ref, cnq_ref, kn_ref, vn_ref, cnk_ref, *refs, ppc):
    del pt_ref
    k_refs, v_refs, f_refs = refs[:ppc], refs[ppc:2 * ppc], refs[2 * ppc:3 * ppc]
    o_ref, m_s, l_s, acc_s, tot_s = refs[3 * ppc:]
    step = pl.program_id(1)

    @pl.when(step == 0)
    def _():
        m_s[...] = jnp.full(m_s.shape, NEG_INF, F32)
        l_s[...] = jnp.zeros(l_s.shape, F32)
        acc_s[...] = jnp.zeros(acc_s.shape, F32)
        tot_s[...] = jnp.zeros(tot_s.shape, F32)

    q = q_ref[0].astype(BF16)
    kc = jnp.concatenate([r[0] for r in k_refs], axis=2).astype(BF16)
    vc = jnp.concatenate([r[0] for r in v_refs], axis=2).astype(BF16)
    n_keys = kc.shape[2]

    logf_t = jnp.concatenate([r[0] for r in f_refs], axis=1)
    later = jnp.where(lax.broadcasted_iota(jnp.int32, (n_keys, n_keys), 0)
                      > lax.broadcasted_iota(jnp.int32, (n_keys, n_keys), 1), 1.0, 0.0)
    rest = jnp.dot(logf_t, later.astype(F32), preferred_element_type=F32,
                   precision=_HIGHEST) + tot_s[...]
    tot_s[...] = tot_s[...] + jnp.sum(logf_t, axis=1, keepdims=True)

    s = jnp.einsum("htd,hdk->htk", q, kc, preferred_element_type=F32)
    s = _add_key_bias(s, rest) + cnq_ref[0]
    _online_update(s, vc, m_s, l_s, acc_s)

    @pl.when(step == pl.num_programs(1) - 1)
    def _():
        kn = kn_ref[0].astype(BF16)
        s2 = jnp.einsum("htd,hdk->htk", q, kn, preferred_element_type=F32)
        s2 = _add_key_bias(s2, -cnk_ref[0]) + cnq_ref[0]
        t = lax.broadcasted_iota(jnp.int32, s2.shape, 1)
        sp = lax.broadcasted_iota(jnp.int32, s2.shape, 2)
        s2 = jnp.where(sp <= t, s2, NEG_INF)
        _online_update(s2, vn_ref[0].astype(BF16), m_s, l_s, acc_s)
        o_ref[0] = acc_s[...] / l_s[...]


def _sample_attention(q, k_new, v_new, c_new, cache_k, cache_v, cache_logf, page_table, seq_len):
    n = q.shape[0]
    n_seq = n // seq_len
    n_pages = page_table.shape[1]
    page = cache_k.shape[1]
    ppc = PAGES_PER_STEP
    assert n_pages % ppc == 0 and seq_len <= LANES
    n_chunks = n_pages // ppc

    cache_kt = cache_k.transpose(0, 2, 3, 1)
    cache_vt = cache_v.transpose(0, 2, 3, 1)
    cache_ft = cache_logf.transpose(0, 2, 1)

    def heads_first(a):
        return a.reshape(n_seq, seq_len, N_HEADS, HEAD_DIM).transpose(0, 2, 1, 3)

    def new_keys(a):
        a = a.reshape(n_seq, seq_len, N_HEADS, HEAD_DIM).transpose(0, 2, 3, 1)
        return jnp.pad(a, ((0, 0), (0, 0), (0, 0), (0, LANES - seq_len)))

    cn = c_new[:, :N_HEADS].reshape(n_seq, seq_len, N_HEADS).transpose(0, 2, 1)
    cnq = cn[..., None]
    cnk = jnp.pad(cn, ((0, 0), (0, 0), (0, LANES - seq_len)))

    def page_map(j):
        return lambda b, c, pt: (pt[b, (n_chunks - 1 - c) * ppc + j], 0, 0, 0)

    def logf_map(j):
        return lambda b, c, pt: (pt[b, (n_chunks - 1 - c) * ppc + j], 0, 0)

    per_seq4 = lambda s2, s3: pl.BlockSpec((1, N_HEADS, s2, s3), lambda b, c, pt: (b, 0, 0, 0))
    kv_page = lambda j: pl.BlockSpec((1, N_HEADS, HEAD_DIM, page), page_map(j))
    o = pl.pallas_call(
        functools.partial(_sample_attn_kernel, ppc=ppc),
        grid_spec=pltpu.PrefetchScalarGridSpec(
            num_scalar_prefetch=1,
            grid=(n_seq, n_chunks),
            in_specs=[per_seq4(seq_len, HEAD_DIM), per_seq4(seq_len, 1),
                      per_seq4(HEAD_DIM, LANES), per_seq4(HEAD_DIM, LANES),
                      pl.BlockSpec((1, N_HEADS, LANES), lambda b, c, pt: (b, 0, 0))]
            + [kv_page(j) for j in range(ppc)] + [kv_page(j) for j in range(ppc)]
            + [pl.BlockSpec((1, N_HEADS, page), logf_map(j)) for j in range(ppc)],
            out_specs=per_seq4(seq_len, HEAD_DIM),
            scratch_shapes=[pltpu.VMEM((N_HEADS, seq_len, 1), F32),
                            pltpu.VMEM((N_HEADS, seq_len, 1), F32),
                            pltpu.VMEM((N_HEADS, seq_len, HEAD_DIM), F32),
                            pltpu.VMEM((N_HEADS, 1), F32)]),
        out_shape=jax.ShapeDtypeStruct((n_seq, N_HEADS, seq_len, HEAD_DIM), F32),
        compiler_params=_params(2),
        name="sample_attention",
    )(page_table, heads_first(q), cnq, new_keys(k_new), new_keys(v_new),
      cnk, *([cache_kt] * ppc), *([cache_vt] * ppc), *([cache_ft] * ppc))
    return o.transpose(0, 2, 1, 3).reshape(n, D_ATT)


def _pad_heads(w):
    d = w.shape[0]
    w = w.reshape(d, N_HEADS, HEAD_DIM)
    return jnp.pad(w, ((0, 0), (0, 0), (0, HEAD_PAD - HEAD_DIM))).reshape(d, N_HEADS * HEAD_PAD)


def _placement_rows(piece_col0, piece_sign, ones_col0):
    rows = np.zeros((N_PARTS * LANES, N_HEADS * HEAD_PAD), np.float32)
    for h in range(N_HEADS):
        for part in range(N_PARTS):
            rows[part * LANES + h, h * HEAD_PAD + piece_col0 + part] = piece_sign
            rows[ONES_LANE, h * HEAD_PAD + ones_col0 + part] = 1.0
    return jnp.asarray(rows, BF16)


def _prepare_weights(norm_a, w_in_a, conv_w_a, w_out_a, norm_kv, w_kv, b_f, norm_b, w_in_b, w_out_b,
                     w_ple_proj, w_ple_gate, norm_f):
    n_a, n_b = w_in_a.shape[0], w_in_b.shape[0]
    scale = HEAD_DIM ** -0.5
    k_rows = _placement_rows(HEAD_DIM, -1.0, HEAD_DIM + N_PARTS)
    q_rows = _placement_rows(HEAD_DIM + N_PARTS, 1.0, HEAD_DIM)
    ple = lambda i: dict(w_gate=w_ple_gate[i].astype(BF16), w_proj=w_ple_proj[i].astype(BF16))
    conv = [dict(g=norm_a[i][None, :], w_in=w_in_a[i].astype(BF16), conv_w=conv_w_a[i],
                 w_out=w_out_a[i].astype(BF16), **ple(i)) for i in range(n_a)]
    w_k, w_v, w_f = w_kv[:, :D_ATT], w_kv[:, D_ATT:2 * D_ATT], w_kv[:, 2 * D_ATT:]
    w_f_pad = jnp.pad(w_f, ((0, 0), (0, LANES - N_HEADS)))
    kv = dict(g_kv=norm_kv[None, :],
              w_kvf=jnp.concatenate([w_k, w_v, w_f_pad], axis=1).astype(BF16),
              w_vf=jnp.concatenate([w_v, w_f_pad], axis=1).astype(BF16),
              b_f=jnp.pad(b_f, (0, LANES - N_HEADS))[None, :],
              b_f_col=b_f[:, None],
              w_kt_aug=jnp.concatenate([_pad_heads(w_k).astype(BF16), k_rows], axis=0).T,
              w_vt=w_v.T.astype(BF16), w_ft=w_f.T.astype(BF16))
    attn = []
    for j in range(n_b):
        w_q = w_in_b[j][:, :D_ATT] * scale
        attn.append(dict(g=norm_b[j][None, :], w_q=w_q.astype(BF16),
                         w_q_aug=jnp.concatenate([_pad_heads(w_q).astype(BF16), q_rows], axis=0),
                         w_z=w_in_b[j][:, D_ATT:].astype(BF16),
                         w_out=w_out_b[j].astype(BF16), **ple(n_a + j)))
    return conv, kv, attn, norm_f[None, :]


def _trunk(x3, p4, conv_state, weights, attend):
    conv_w, kv_w, attn_w, g_final = weights
    n_seq, seq_len, d = x3.shape
    n = n_seq * seq_len
    long_seq = conv_state is None
    x = x3.reshape(n, d)
    p = p4.reshape(p4.shape[0], n, p4.shape[-1])
    new_conv = []
    for i, w in enumerate(conv_w):
        x, st = _conv_layer(x, p[i], None if long_seq else conv_state[i], w, seq_len)
        new_conv.append(st)
    kv = _shared_kv(x, kv_w, n_seq, seq_len, long_seq)
    for j, w in enumerate(attn_w):
        q, sz = _attn_in(x, kv["c"], w, long_seq)
        o = attend(q, kv)
        x = _attn_out(x, o, sz, p[len(conv_w) + j], w, g_final, final=(j == len(attn_w) - 1))
    return x.reshape(n_seq, seq_len, d), jnp.stack(new_conv), kv["k"], kv["v"], kv["logf"]


def kernel(x_prompt, x_sample, p_prompt, p_sample, state_conv, cache_k, cache_v, cache_logf, page_table, norm_a, w_in_a, conv_w_a, w_out_a, norm_kv, w_kv, b_f, norm_b, w_in_b, w_out_b, w_ple_proj, w_ple_gate, norm_f):
    assert conv_w_a.shape[1] == CONV_W and w_kv.shape[1] == 2 * D_ATT + N_HEADS
    weights = _prepare_weights(norm_a, w_in_a, conv_w_a, w_out_a, norm_kv, w_kv, b_f, norm_b, w_in_b,
                               w_out_b, w_ple_proj, w_ple_gate, norm_f)
    n_prompt, prompt_len = x_prompt.shape[:2]
    sample_len = x_sample.shape[1]

    def prompt_attend(qa, kv):
        return _prompt_attention(qa, kv["ka"], kv["vb"], n_prompt, prompt_len)

    def sample_attend(q, kv):
        return _sample_attention(q, kv["k2"], kv["v2"], kv["c"], cache_k, cache_v, cache_logf,
                                 page_table, sample_len)

    y_p, conv_p, k_p, v_p, logf_p = _trunk(x_prompt, p_prompt, None, weights, prompt_attend)
    y_s, conv_s, k_s, v_s, logf_s = _trunk(x_sample, p_sample, state_conv, weights, sample_attend)
    return (y_p, y_s, conv_p, k_p, v_p, logf_p, conv_s, k_s, v_s, logf_s)
```

```python
import functools

import numpy as np
import jax
import jax.numpy as jnp
from jax import lax
from jax.experimental import pallas as pl
from jax.experimental.pallas import tpu as pltpu

F32 = jnp.float32
BF16 = jnp.bfloat16

N_HEADS = 16
HEAD_DIM = 64
D_ATT = N_HEADS * HEAD_DIM
CONV_W = 3
EPS = 1e-6
NEG_INF = -1e30

LANES = 128
SUBLANES = 8
HEAD_PAD = LANES
N_PARTS = 3
ONES_LANE = N_HEADS
BF16_ROWS = 16
V_ROWS = HEAD_DIM + BF16_ROWS
VMEM_LIMIT_BYTES = 56 * 2**20

ROW_TILE = 512
CONV_ROW_TILE = 256
Q_TILE = 512
PAGES_PER_STEP = 8

_HIGHEST = lax.Precision.HIGHEST


def _params(n_grid_dims):
    return pltpu.CompilerParams(
        dimension_semantics=("arbitrary",) * n_grid_dims,
        vmem_limit_bytes=VMEM_LIMIT_BYTES)


def _resident(shape):
    zeros = (0,) * len(shape)
    return pl.BlockSpec(shape, lambda *_: zeros, pipeline_mode=pl.Buffered(1))


def _rows(tm, width):
    return pl.BlockSpec((tm, width), lambda i: (i, 0))


def _dot(a, b):
    return jnp.dot(a, b, preferred_element_type=F32)


def _rmsnorm(x, g):
    return x * lax.rsqrt(jnp.mean(x * x, axis=-1, keepdims=True) + EPS) * g


def _silu(z):
    return z * jax.nn.sigmoid(z)


def _log_sigmoid(x):
    return jnp.minimum(x, 0.0) - jnp.log1p(jnp.exp(-jnp.abs(x)))


def _ple_residual(x1, p_ref, wg_ref, wp_ref):
    gate = jax.nn.sigmoid(_dot(x1.astype(BF16), wg_ref[...]))
    proj = _dot(p_ref[...].astype(BF16), wp_ref[...])
    return x1 + gate * proj


def _seq_index(idx, seq_len):
    if seq_len & (seq_len - 1) == 0:
        shift = seq_len.bit_length() - 1
        return lax.shift_right_logical(idx, shift), idx & (seq_len - 1)
    return idx // seq_len, idx % seq_len


def _split_bf16(c):
    hi = c.astype(BF16)
    r1 = c - hi.astype(F32)
    mid = r1.astype(BF16)
    lo = (r1 - mid.astype(F32)).astype(BF16)
    return hi, mid, lo


def _bias_pieces(c):
    lane = lax.broadcasted_iota(jnp.int32, c.shape, 1)
    hi, mid, lo = _split_bf16(jnp.where(lane < N_HEADS, c, 0.0))
    hi = jnp.where(lane == ONES_LANE, 1.0, hi).astype(BF16)
    return jnp.concatenate([hi, mid, lo], axis=1)


def _conv_core(x_ref, p_ref, g_ref, win_ref, cw_ref, wout_ref, wg_ref, wp_ref, xo_ref, ubuf,
               fix_first_rows):
    tm, d = x_ref.shape
    x = x_ref[...]
    h = _rmsnorm(x, g_ref[...]).astype(BF16)
    proj = _dot(h, win_ref[...])
    b_gate, c_gate, xin, z = (proj[:, k * d:(k + 1) * d] for k in range(4))
    u = c_gate * xin
    ubuf[SUBLANES:SUBLANES + tm, :] = u
    u1 = ubuf[SUBLANES - 1:SUBLANES - 1 + tm, :]
    u2 = ubuf[SUBLANES - 2:SUBLANES - 2 + tm, :]
    u1, u2 = fix_first_rows(u1, u2)
    cw = cw_ref[...]
    conv = cw[0:1, :] * u2 + cw[1:2, :] * u1 + cw[2:3, :] * u
    y = b_gate * conv * _silu(z)
    x1 = x + _dot(y.astype(BF16), wout_ref[...])
    xo_ref[...] = _ple_residual(x1, p_ref, wg_ref, wp_ref)
    return u


def _conv_long_kernel(x_ref, p_ref, g_ref, win_ref, cw_ref, wout_ref, wg_ref, wp_ref,
                      xo_ref, tail_ref, ubuf, *, tiles_per_seq):
    tm, d = x_ref.shape
    first = (pl.program_id(0) % tiles_per_seq) == 0

    @pl.when(first)
    def _():
        ubuf[0:SUBLANES, :] = jnp.zeros((SUBLANES, d), F32)

    @pl.when(jnp.logical_not(first))
    def _():
        ubuf[0:SUBLANES, :] = ubuf[tm:tm + SUBLANES, :]

    _conv_core(x_ref, p_ref, g_ref, win_ref, cw_ref, wout_ref, wg_ref, wp_ref, xo_ref, ubuf,
               lambda u1, u2: (u1, u2))
    tail_ref[0] = ubuf[tm:tm + SUBLANES, :]


def _conv_short_kernel(x_ref, p_ref, s1_ref, s2_ref, g_ref, win_ref, cw_ref, wout_ref, wg_ref,
                       wp_ref, xo_ref, u_ref, ubuf, *, seq_len):
    tm, d = x_ref.shape
    ubuf[0:SUBLANES, :] = jnp.zeros((SUBLANES, d), F32)
    _, t = _seq_index(lax.broadcasted_iota(jnp.int32, (tm, d), 0), seq_len)

    def fix(u1, u2):
        return (jnp.where(t < 1, s1_ref[...], u1), jnp.where(t < 2, s2_ref[...], u2))

    u_ref[...] = _conv_core(x_ref, p_ref, g_ref, win_ref, cw_ref, wout_ref, wg_ref, wp_ref,
                            xo_ref, ubuf, fix)


def _conv_layer(x, p, state, w, seq_len):
    n, d = x.shape
    dp = p.shape[1]
    n_seq = n // seq_len
    weights = (w["g"], w["w_in"], w["conv_w"], w["w_out"], w["w_gate"], w["w_proj"])
    w_specs = [_resident(a.shape) for a in weights]
    if state is None:
        tm = min(CONV_ROW_TILE, seq_len)
        assert seq_len % tm == 0 and tm % SUBLANES == 0
        x_new, tails = pl.pallas_call(
            functools.partial(_conv_long_kernel, tiles_per_seq=seq_len // tm),
            grid=(n // tm,),
            in_specs=[_rows(tm, d), _rows(tm, dp)] + w_specs,
            out_specs=[_rows(tm, d), pl.BlockSpec((1, SUBLANES, d), lambda i: (i, 0, 0))],
            out_shape=[jax.ShapeDtypeStruct((n, d), F32),
                       jax.ShapeDtypeStruct((n // tm, SUBLANES, d), F32)],
            scratch_shapes=[pltpu.VMEM((tm + SUBLANES, d), F32)],
            compiler_params=_params(1),
            name="conv_layer_prompt",
        )(x, p, *weights)
        last = tails.reshape(n_seq, seq_len // tm, SUBLANES, d)[:, -1]
        return x_new, last[:, SUBLANES - (CONV_W - 1):, :]
    tm = n
    assert tm % seq_len == 0 and seq_len >= CONV_W - 1
    zeros = jnp.zeros((n_seq, seq_len, d), F32)
    s1 = zeros.at[:, 0].set(state[:, 1]).reshape(n, d)
    s2 = zeros.at[:, 0].set(state[:, 0]).at[:, 1].set(state[:, 1]).reshape(n, d)
    x_new, u = pl.pallas_call(
        functools.partial(_conv_short_kernel, seq_len=seq_len),
        grid=(n // tm,),
        in_specs=[_rows(tm, d), _rows(tm, dp), _rows(tm, d), _rows(tm, d)] + w_specs,
        out_specs=[_rows(tm, d), _rows(tm, d)],
        out_shape=[jax.ShapeDtypeStruct((n, d), F32), jax.ShapeDtypeStruct((n, d), F32)],
        scratch_shapes=[pltpu.VMEM((tm + SUBLANES, d), F32)],
        compiler_params=_params(1),
        name="conv_layer_sample",
    )(x, p, s1, s2, *weights)
    return x_new, u.reshape(n_seq, seq_len, d)[:, seq_len - (CONV_W - 1):, :]


def _cumsum_rows(vals, seq_len):
    tm = vals.shape[0]
    r = lax.broadcasted_iota(jnp.int32, (tm, tm), 0)
    c = lax.broadcasted_iota(jnp.int32, (tm, tm), 1)
    if seq_len < tm:
        lower = jnp.where(c <= r, _seq_index(c, seq_len)[0], -1)
        tri = jnp.where(lower == _seq_index(r, seq_len)[0], 1.0, 0.0)
    else:
        tri = jnp.where(c <= r, 1.0, 0.0)
    return jnp.dot(tri.astype(F32), vals, preferred_element_type=F32, precision=_HIGHEST)


def _dot_nt(a, b):
    return lax.dot_general(a, b, (((1,), (1,)), ((), ())), preferred_element_type=F32)


def _kv_long_kernel(x_ref, g_ref, wf_ref, bf_ref, wka_ref, wkt_ref, wvt_ref, wft_ref, bfc_ref,
                    kt_ref, vt_ref, logft_ref, c_ref, ka_ref, vtb_ref, carry, *, tiles_per_seq):
    tm = x_ref.shape[0]

    @pl.when((pl.program_id(0) % tiles_per_seq) == 0)
    def _():
        carry[...] = jnp.zeros(carry.shape, F32)

    h = _rmsnorm(x_ref[...], g_ref[...]).astype(BF16)
    logf = _log_sigmoid(_dot(h, wf_ref[...]) + bf_ref[...])
    c = _cumsum_rows(logf, tm) + carry[0:1, :]
    carry[...] = jnp.broadcast_to(c[tm - 1:tm, :], carry.shape)
    c_ref[...] = c
    lhs = jnp.concatenate([h, _bias_pieces(c)], axis=1)
    ka_ref[...] = _dot(lhs, wka_ref[...]).astype(BF16)
    kt_ref[0] = _dot_nt(wkt_ref[...], h)
    vt = _dot_nt(wvt_ref[...], h)
    vt_ref[0] = vt
    ones_row = jnp.where(lax.broadcasted_iota(jnp.int32, (BF16_ROWS, tm), 0) == 0, 1.0, 0.0).astype(BF16)
    for hd in range(N_HEADS):
        vtb_ref[0, hd * V_ROWS:hd * V_ROWS + HEAD_DIM, :] = (
            vt[hd * HEAD_DIM:(hd + 1) * HEAD_DIM, :].astype(BF16))
        vtb_ref[0, hd * V_ROWS + HEAD_DIM:(hd + 1) * V_ROWS, :] = ones_row
    logft_ref[0] = _log_sigmoid(_dot_nt(wft_ref[...], h) + bfc_ref[...])


def _kv_short_kernel(x_ref, g_ref, wa_ref, bf_ref, k_ref, v_ref, logf_ref, c_ref, *, seq_len):
    h = _rmsnorm(x_ref[...], g_ref[...]).astype(BF16)
    r = _dot(h, wa_ref[...])
    k_ref[...] = r[:, :D_ATT]
    v_ref[...] = r[:, D_ATT:2 * D_ATT]
    logf = _log_sigmoid(r[:, 2 * D_ATT:] + bf_ref[...])
    logf_ref[...] = logf[:, :N_HEADS]
    c_ref[...] = _cumsum_rows(logf, seq_len)


def _shared_kv(x, w, n_seq, seq_len, long_seq):
    n, d = x.shape
    if long_seq:
        tm = min(ROW_TILE, seq_len)
        assert seq_len % tm == 0
        tps = seq_len // tm
        weights = (w["g_kv"], w["w_f"], w["b_f"], w["w_k_aug"], w["w_kt"], w["w_vt"], w["w_ft"],
                   w["b_f_col"])
        by_seq = lambda rows: pl.BlockSpec((1, rows, tm), lambda i: (i // tps, 0, i % tps))
        kt, vt, logft, c, ka, vtb = pl.pallas_call(
            functools.partial(_kv_long_kernel, tiles_per_seq=tps),
            grid=(n // tm,),
            in_specs=[_rows(tm, d)] + [_resident(a.shape) for a in weights],
            out_specs=[by_seq(D_ATT), by_seq(D_ATT), by_seq(N_HEADS), _rows(tm, LANES),
                       _rows(tm, N_HEADS * HEAD_PAD), by_seq(N_HEADS * V_ROWS)],
            out_shape=[jax.ShapeDtypeStruct((n_seq, D_ATT, seq_len), F32),
                       jax.ShapeDtypeStruct((n_seq, D_ATT, seq_len), F32),
                       jax.ShapeDtypeStruct((n_seq, N_HEADS, seq_len), F32),
                       jax.ShapeDtypeStruct((n, LANES), F32),
                       jax.ShapeDtypeStruct((n, N_HEADS * HEAD_PAD), BF16),
                       jax.ShapeDtypeStruct((n_seq, N_HEADS * V_ROWS, seq_len), BF16)],
            scratch_shapes=[pltpu.VMEM((SUBLANES, LANES), F32)],
            compiler_params=_params(1),
            name="shared_kv_prompt",
        )(x, *weights)
        to_std = lambda a: a.reshape(n_seq, N_HEADS, HEAD_DIM, seq_len).transpose(0, 3, 1, 2)
        return dict(k=to_std(kt), v=to_std(vt), logf=logft.transpose(0, 2, 1), c=c, ka=ka, vtb=vtb)
    tm = n
    assert tm % seq_len == 0
    weights = (w["g_kv"], w["w_kvf"], w["b_f"])
    k, v, logf, c = pl.pallas_call(
        functools.partial(_kv_short_kernel, seq_len=seq_len),
        grid=(n // tm,),
        in_specs=[_rows(tm, d)] + [_resident(a.shape) for a in weights],
        out_specs=[_rows(tm, D_ATT), _rows(tm, D_ATT), _rows(tm, N_HEADS), _rows(tm, LANES)],
        out_shape=[jax.ShapeDtypeStruct((n, D_ATT), F32), jax.ShapeDtypeStruct((n, D_ATT), F32),
                   jax.ShapeDtypeStruct((n, N_HEADS), F32), jax.ShapeDtypeStruct((n, LANES), F32)],
        compiler_params=_params(1),
        name="shared_kv_sample",
    )(x, *weights)
    std = lambda a: a.reshape(n_seq, seq_len, N_HEADS, HEAD_DIM)
    return dict(k=std(k), v=std(v), logf=logf.reshape(n_seq, seq_len, N_HEADS), c=c, k2=k, v2=v)


def _attn_in_prompt_kernel(x_ref, c_ref, g_ref, wqt_ref, wz_ref, qt_ref, sz_ref):
    h = _rmsnorm(x_ref[...], g_ref[...]).astype(BF16)
    lhs = jnp.concatenate([h, _bias_pieces(c_ref[...])], axis=1)
    qt_ref[0] = _dot_nt(wqt_ref[...], lhs).astype(BF16)
    sz_ref[...] = _silu(_dot(h, wz_ref[...]))


def _attn_in_sample_kernel(x_ref, g_ref, wq_ref, wz_ref, q_ref, sz_ref):
    h = _rmsnorm(x_ref[...], g_ref[...]).astype(BF16)
    q_ref[...] = _dot(h, wq_ref[...])
    sz_ref[...] = _silu(_dot(h, wz_ref[...]))


def _attn_in(x, c, w, n_seq, seq_len, long_seq):
    n, d = x.shape
    if long_seq:
        tm = min(ROW_TILE, seq_len)
        assert seq_len % tm == 0
        tps = seq_len // tm
        weights = (w["g"], w["w_qt_aug"], w["w_z"])
        return pl.pallas_call(
            _attn_in_prompt_kernel,
            grid=(n // tm,),
            in_specs=[_rows(tm, d), _rows(tm, LANES)] + [_resident(a.shape) for a in weights],
            out_specs=[pl.BlockSpec((1, N_HEADS * HEAD_PAD, tm), lambda i: (i // tps, 0, i % tps)),
                       _rows(tm, D_ATT)],
            out_shape=[jax.ShapeDtypeStruct((n_seq, N_HEADS * HEAD_PAD, seq_len), BF16),
                       jax.ShapeDtypeStruct((n, D_ATT), F32)],
            compiler_params=_params(1),
            name="attn_in_prompt",
        )(x, c, *weights)
    tm = min(ROW_TILE, n)
    weights = (w["g"], w["w_q"], w["w_z"])
    return pl.pallas_call(
        _attn_in_sample_kernel,
        grid=(n // tm,),
        in_specs=[_rows(tm, d)] + [_resident(a.shape) for a in weights],
        out_specs=[_rows(tm, D_ATT), _rows(tm, D_ATT)],
        out_shape=[jax.ShapeDtypeStruct((n, D_ATT), F32), jax.ShapeDtypeStruct((n, D_ATT), F32)],
        compiler_params=_params(1),
        name="attn_in_sample",
    )(x, *weights)


def _attn_out_kernel(x_ref, o_ref, sz_ref, p_ref, wout_ref, wg_ref, wp_ref, gf_ref, out_ref, *,
                     final):
    y = (o_ref[...] * sz_ref[...]).astype(BF16)
    x1 = x_ref[...] + _dot(y, wout_ref[...])
    x_new = _ple_residual(x1, p_ref, wg_ref, wp_ref)
    out_ref[...] = _rmsnorm(x_new, gf_ref[...]) if final else x_new


def _attn_out(x, o, sz, p, w, g_final, final):
    n, d = x.shape
    tm = min(ROW_TILE, n)
    weights = (w["w_out"], w["w_gate"], w["w_proj"], g_final)
    return pl.pallas_call(
        functools.partial(_attn_out_kernel, final=final),
        grid=(n // tm,),
        in_specs=[_rows(tm, d), _rows(tm, D_ATT), _rows(tm, D_ATT), _rows(tm, p.shape[1])]
        + [_resident(a.shape) for a in weights],
        out_specs=_rows(tm, d),
        out_shape=jax.ShapeDtypeStruct((n, d), F32),
        compiler_params=_params(1),
        name="attn_out",
    )(x, o, sz, p, *weights)


def _flash_kernel(qt_ref, ka_ref, vt_ref, o_ref, s_scr, mt_scr, m_scr, acc_scr, *, tile):
    i = pl.program_id(2)
    qts = [qt_ref[0, hh * HEAD_PAD:(hh + 1) * HEAD_PAD, :] for hh in range(2)]

    def logits(j, slot, mask):
        off = pl.multiple_of(j * tile, tile)
        for hh in range(2):
            s = _dot(ka_ref[0, pl.ds(off, tile), hh * HEAD_PAD:(hh + 1) * HEAD_PAD], qts[hh])
            if mask is not None:
                s = jnp.where(mask, s, NEG_INF)
            s_scr[slot, hh] = s
            mt_scr[slot, hh] = jnp.max(s, axis=0, keepdims=True)

    def fold(j, slot):
        off = pl.multiple_of(j * tile, tile)
        for hh in range(2):
            m = m_scr[hh]
            m_new = jnp.maximum(m, mt_scr[slot, hh])
            p = jnp.exp(s_scr[slot, hh] - m_new).astype(BF16)
            vt = vt_ref[0, hh * V_ROWS:(hh + 1) * V_ROWS, pl.ds(off, tile)]
            acc_scr[hh] = jnp.exp(m - m_new) * acc_scr[hh] + _dot(vt, p)
            m_scr[hh] = m_new

    m_scr[...] = jnp.full(m_scr.shape, NEG_INF, F32)
    acc_scr[...] = jnp.zeros(acc_scr.shape, F32)
    key = lax.broadcasted_iota(jnp.int32, (tile, tile), 0)
    qry = lax.broadcasted_iota(jnp.int32, (tile, tile), 1)
    logits(i, 0, key <= qry)
    n_tiles = i + 1
    visit = lambda k: jnp.where(k == 0, i, k - 1)

    def two_steps(u, _):
        k = 2 * u
        logits(visit(k + 1), 1, None)
        fold(visit(k), 0)
        logits(visit(k + 2), 0, None)
        fold(visit(k + 1), 1)
        return 0

    n_pairs = (n_tiles - 1) // 2
    lax.fori_loop(0, n_pairs, two_steps, 0)
    k = 2 * n_pairs

    @pl.when(n_tiles - k == 2)
    def _():
        logits(visit(k + 1), 1, None)
        fold(visit(k), 0)
        fold(visit(k + 1), 1)

    @pl.when(n_tiles - k == 1)
    def _():
        fold(visit(k), 0)

    o_t = jnp.concatenate(
        [acc_scr[hh, :HEAD_DIM, :] / acc_scr[hh, HEAD_DIM:HEAD_DIM + 1, :] for hh in range(2)], axis=0)
    o_ref[0] = o_t.T


def _prompt_attention(qt, ka, vtb, n_seq, seq_len):
    tile = min(Q_TILE, seq_len)
    assert seq_len % tile == 0
    pair = 2 * HEAD_PAD
    ka3 = ka.reshape(n_seq, seq_len, N_HEADS * HEAD_PAD)
    o = pl.pallas_call(
        functools.partial(_flash_kernel, tile=tile),
        grid=(n_seq, N_HEADS // 2, seq_len // tile),
        in_specs=[pl.BlockSpec((1, pair, tile), lambda b, hp, i: (b, hp, i)),
                  pl.BlockSpec((1, seq_len, pair), lambda b, hp, i: (b, 0, hp)),
                  pl.BlockSpec((1, 2 * V_ROWS, seq_len), lambda b, hp, i: (b, hp, 0))],
        out_specs=pl.BlockSpec((1, tile, 2 * HEAD_DIM), lambda b, hp, i: (b, i, hp)),
        out_shape=jax.ShapeDtypeStruct((n_seq, seq_len, D_ATT), F32),
        scratch_shapes=[pltpu.VMEM((2, 2, tile, tile), F32),
                        pltpu.VMEM((2, 2, 1, tile), F32),
                        pltpu.VMEM((2, 1, tile), F32),
                        pltpu.VMEM((2, V_ROWS, tile), F32)],
        compiler_params=_params(3),
        name="prompt_attention",
    )(qt, ka3, vtb)
    return o.reshape(n_seq * seq_len, D_ATT)


def _online_update(s, pv_of, m_s, l_s, acc_s):
    m_prev = m_s[...]
    m_new = jnp.maximum(m_prev, jnp.max(s, axis=-1, keepdims=True))
    alpha = jnp.exp(m_prev - m_new)
    p = jnp.exp(s - m_new)
    l_s[...] = alpha * l_s[...] + jnp.sum(p, axis=-1, keepdims=True)
    acc_s[...] = alpha * acc_s[...] + pv_of(p.astype(BF16))
    m_s[...] = m_new


def _suffix_sum_lanes(x):
    n = x.shape[1]
    lane = lax.broadcasted_iota(jnp.int32, x.shape, 1)
    y = jnp.where(lane + 1 < n, pltpu.roll(x, n - 1, axis=1), 0.0)
    shift = 1
    while shift < n:
        y = y + jnp.where(lane + shift < n, pltpu.roll(y, n - shift, axis=1), 0.0)
        shift *= 2
    return y


def _per_head_rows(bias, seq_len):
    return jnp.concatenate(
        [jnp.broadcast_to(bias[h:h + 1, :], (seq_len, bias.shape[1])) for h in range(N_HEADS)], axis=0)


def _sample_attn_kernel(pt_ref, q_ref, cnq_ref, kn_ref, vn_ref, cnk_ref, *refs, ppc):
    del pt_ref
    k_refs, v_refs, f_refs = refs[:ppc], refs[ppc:2 * ppc], refs[2 * ppc:3 * ppc]
    o_ref, qbd_s, m_s, l_s, acc_s, tot_s = refs[3 * ppc:]
    seq_len = q_ref.shape[0]
    rows = N_HEADS * seq_len
    step = pl.program_id(1)
    row_head = _seq_index(lax.broadcasted_iota(jnp.int32, (rows, D_ATT), 0), seq_len)[0]
    own_head = row_head == _seq_index(lax.broadcasted_iota(jnp.int32, (rows, D_ATT), 1), HEAD_DIM)[0]

    @pl.when(step == 0)
    def _():
        q_rows = jnp.concatenate([q_ref[...]] * N_HEADS, axis=0)
        qbd_s[...] = jnp.where(own_head, q_rows, 0.0).astype(BF16)
        m_s[...] = jnp.full(m_s.shape, NEG_INF, F32)
        l_s[...] = jnp.zeros(l_s.shape, F32)
        acc_s[...] = jnp.zeros(acc_s.shape, F32)
        tot_s[...] = jnp.zeros(tot_s.shape, F32)

    page = k_refs[0].shape[3]
    kc = jnp.concatenate([r[0].reshape(D_ATT, page) for r in k_refs], axis=1).astype(BF16)
    vc = jnp.concatenate([r[0].reshape(D_ATT, page) for r in v_refs], axis=1).astype(BF16)

    after = tot_s[...]
    rest_pages = [None] * ppc
    for pg in reversed(range(ppc)):
        logf_pg = f_refs[pg][0]
        rest_pages[pg] = _suffix_sum_lanes(logf_pg) + after
        after = after + jnp.sum(logf_pg, axis=1, keepdims=True)
    tot_s[...] = after
    rest = jnp.concatenate(rest_pages, axis=1)

    s = _dot(qbd_s[...], kc) + _per_head_rows(rest, seq_len) + cnq_ref[0]
    _online_update(s, lambda p: _dot_nt(p, vc), m_s, l_s, acc_s)

    @pl.when(step == pl.num_programs(1) - 1)
    def _():
        no_keys = jnp.zeros((LANES - seq_len, D_ATT), F32)
        kn = jnp.concatenate([kn_ref[...], no_keys], axis=0).astype(BF16)
        vn = jnp.concatenate([vn_ref[...], no_keys], axis=0).astype(BF16)
        s2 = _dot_nt(qbd_s[...], kn) - _per_head_rows(cnk_ref[0], seq_len) + cnq_ref[0]
        t = _seq_index(lax.broadcasted_iota(jnp.int32, s2.shape, 0), seq_len)[1]
        sp = lax.broadcasted_iota(jnp.int32, s2.shape, 1)
        s2 = jnp.where(sp <= t, s2, NEG_INF)
        _online_update(s2, lambda p: _dot(p, vn), m_s, l_s, acc_s)
        o_blocks = jnp.where(own_head, acc_s[...] / l_s[...], 0.0)
        o_ref[...] = functools.reduce(
            lambda a, b: a + b, [o_blocks[h * seq_len:(h + 1) * seq_len, :] for h in range(N_HEADS)])


def _sample_attention(q, k_new, v_new, c_new, cache_k, cache_v, cache_logf, page_table, seq_len):
    n = q.shape[0]
    n_seq = n // seq_len
    n_pages = page_table.shape[1]
    page = cache_k.shape[1]
    ppc = PAGES_PER_STEP
    assert n_pages % ppc == 0 and seq_len <= LANES
    n_chunks = n_pages // ppc

    cache_kt = cache_k.transpose(0, 2, 3, 1)
    cache_vt = cache_v.transpose(0, 2, 3, 1)
    cache_ft = cache_logf.transpose(0, 2, 1)

    rows = N_HEADS * seq_len
    cn = c_new[:, :N_HEADS].reshape(n_seq, seq_len, N_HEADS).transpose(0, 2, 1)
    cnq = cn.reshape(n_seq, rows, 1)
    cnk = jnp.pad(cn, ((0, 0), (0, 0), (0, LANES - seq_len)))

    def page_map(j):
        return lambda b, c, pt: (pt[b, (n_chunks - 1 - c) * ppc + j], 0, 0, 0)

    def logf_map(j):
        return lambda b, c, pt: (pt[b, (n_chunks - 1 - c) * ppc + j], 0, 0)

    seq_rows = pl.BlockSpec((seq_len, D_ATT), lambda b, c, pt: (b, 0))
    kv_page = lambda j: pl.BlockSpec((1, N_HEADS, HEAD_DIM, page), page_map(j))
    return pl.pallas_call(
        functools.partial(_sample_attn_kernel, ppc=ppc),
        grid_spec=pltpu.PrefetchScalarGridSpec(
            num_scalar_prefetch=1,
            grid=(n_seq, n_chunks),
            in_specs=[seq_rows, pl.BlockSpec((1, rows, 1), lambda b, c, pt: (b, 0, 0)),
                      seq_rows, seq_rows,
                      pl.BlockSpec((1, N_HEADS, LANES), lambda b, c, pt: (b, 0, 0))]
            + [kv_page(j) for j in range(ppc)] + [kv_page(j) for j in range(ppc)]
            + [pl.BlockSpec((1, N_HEADS, page), logf_map(j)) for j in range(ppc)],
            out_specs=seq_rows,
            scratch_shapes=[pltpu.VMEM((rows, D_ATT), BF16),
                            pltpu.VMEM((rows, 1), F32), pltpu.VMEM((rows, 1), F32),
                            pltpu.VMEM((rows, D_ATT), F32),
                            pltpu.VMEM((N_HEADS, 1), F32)]),
        out_shape=jax.ShapeDtypeStruct((n, D_ATT), F32),
        compiler_params=_params(2),
        name="sample_attention",
    )(page_table, q, cnq, k_new, v_new,
      cnk, *([cache_kt] * ppc), *([cache_vt] * ppc), *([cache_ft] * ppc))


def _pad_heads(w):
    d = w.shape[0]
    w = w.reshape(d, N_HEADS, HEAD_DIM)
    return jnp.pad(w, ((0, 0), (0, 0), (0, HEAD_PAD - HEAD_DIM))).reshape(d, N_HEADS * HEAD_PAD)


def _placement_rows(piece_col0, piece_sign, ones_col0):
    rows = np.zeros((N_PARTS * LANES, N_HEADS * HEAD_PAD), np.float32)
    for h in range(N_HEADS):
        for part in range(N_PARTS):
            rows[part * LANES + h, h * HEAD_PAD + piece_col0 + part] = piece_sign
            rows[ONES_LANE, h * HEAD_PAD + ones_col0 + part] = 1.0
    return jnp.asarray(rows, BF16)


def _prepare_weights(norm_a, w_in_a, conv_w_a, w_out_a, norm_kv, w_kv, b_f, norm_b, w_in_b, w_out_b,
                     w_ple_proj, w_ple_gate, norm_f):
    n_a, n_b = w_in_a.shape[0], w_in_b.shape[0]
    scale = HEAD_DIM ** -0.5
    k_rows = _placement_rows(HEAD_DIM, -1.0, HEAD_DIM + N_PARTS)
    q_rows = _placement_rows(HEAD_DIM + N_PARTS, 1.0, HEAD_DIM)
    ple = lambda i: dict(w_gate=w_ple_gate[i].astype(BF16), w_proj=w_ple_proj[i].astype(BF16))
    conv = [dict(g=norm_a[i][None, :], w_in=w_in_a[i].astype(BF16), conv_w=conv_w_a[i],
                 w_out=w_out_a[i].astype(BF16), **ple(i)) for i in range(n_a)]
    w_k, w_v, w_f = w_kv[:, :D_ATT], w_kv[:, D_ATT:2 * D_ATT], w_kv[:, 2 * D_ATT:]
    w_f_pad = jnp.pad(w_f, ((0, 0), (0, LANES - N_HEADS)))
    kv = dict(g_kv=norm_kv[None, :],
              w_kvf=jnp.concatenate([w_k, w_v, w_f_pad], axis=1).astype(BF16),
              w_f=w_f_pad.astype(BF16),
              b_f=jnp.pad(b_f, (0, LANES - N_HEADS))[None, :],
              b_f_col=b_f[:, None],
              w_k_aug=jnp.concatenate([_pad_heads(w_k).astype(BF16), k_rows], axis=0),
              w_kt=w_k.T.astype(BF16), w_vt=w_v.T.astype(BF16), w_ft=w_f.T.astype(BF16))
    attn = []
    for j in range(n_b):
        w_q = w_in_b[j][:, :D_ATT] * scale
        attn.append(dict(g=norm_b[j][None, :], w_q=w_q.astype(BF16),
                         w_qt_aug=jnp.concatenate([_pad_heads(w_q).astype(BF16), q_rows], axis=0).T,
                         w_z=w_in_b[j][:, D_ATT:].astype(BF16),
                         w_out=w_out_b[j].astype(BF16), **ple(n_a + j)))
    return conv, kv, attn, norm_f[None, :]


def _trunk(x3, p4, conv_state, weights, attend):
    conv_w, kv_w, attn_w, g_final = weights
    n_seq, seq_len, d = x3.shape
    n = n_seq * seq_len
    long_seq = conv_state is None
    x = x3.reshape(n, d)
    p = p4.reshape(p4.shape[0], n, p4.shape[-1])
    new_conv = []
    for i, w in enumerate(conv_w):
        x, st = _conv_layer(x, p[i], None if long_seq else conv_state[i], w, seq_len)
        new_conv.append(st)
    kv = _shared_kv(x, kv_w, n_seq, seq_len, long_seq)
    for j, w in enumerate(attn_w):
        q, sz = _attn_in(x, kv["c"], w, n_seq, seq_len, long_seq)
        o = attend(q, kv)
        x = _attn_out(x, o, sz, p[len(conv_w) + j], w, g_final, final=(j == len(attn_w) - 1))
    return x.reshape(n_seq, seq_len, d), jnp.stack(new_conv), kv["k"], kv["v"], kv["logf"]


def kernel(x_prompt, x_sample, p_prompt, p_sample, state_conv, cache_k, cache_v, cache_logf, page_table, norm_a, w_in_a, conv_w_a, w_out_a, norm_kv, w_kv, b_f, norm_b, w_in_b, w_out_b, w_ple_proj, w_ple_gate, norm_f):
    assert conv_w_a.shape[1] == CONV_W and w_kv.shape[1] == 2 * D_ATT + N_HEADS
    weights = _prepare_weights(norm_a, w_in_a, conv_w_a, w_out_a, norm_kv, w_kv, b_f, norm_b, w_in_b,
                               w_out_b, w_ple_proj, w_ple_gate, norm_f)
    n_prompt, prompt_len = x_prompt.shape[:2]
    sample_len = x_sample.shape[1]

    def prompt_attend(qt, kv):
        return _prompt_attention(qt, kv["ka"], kv["vtb"], n_prompt, prompt_len)

    def sample_attend(q, kv):
        return _sample_attention(q, kv["k2"], kv["v2"], kv["c"], cache_k, cache_v, cache_logf,
                                 page_table, sample_len)

    y_p, conv_p, k_p, v_p, logf_p = _trunk(x_prompt, p_prompt, None, weights, prompt_attend)
    y_s, conv_s, k_s, v_s, logf_s = _trunk(x_sample, p_sample, state_conv, weights, sample_attend)
    return (y_p, y_s, conv_p, k_p, v_p, logf_p, conv_s, k_s, v_s, logf_s)
```

```python
import functools

import numpy as np
import jax
import jax.numpy as jnp
from jax import lax
from jax.experimental import pallas as pl
from jax.experimental.pallas import tpu as pltpu

F32 = jnp.float32
BF16 = jnp.bfloat16

N_HEADS = 16
HEAD_DIM = 64
D_ATT = N_HEADS * HEAD_DIM
CONV_W = 3
EPS = 1e-6
NEG_INF = -1e30
LOG2_E = 1.4426950408889634

LANES = 128
SUBLANES = 8
HEAD_PAD = LANES
N_PARTS = 3
BF16_ROWS = 16
V_ROWS = HEAD_DIM + BF16_ROWS
VMEM_LIMIT_BYTES = 56 * 2**20

ROW_TILE = 512
CONV_ROW_TILE = 256
Q_TILE = 512
PAGES_PER_STEP = 8

_HIGHEST = lax.Precision.HIGHEST


def _params(n_grid_dims):
    return pltpu.CompilerParams(
        dimension_semantics=("arbitrary",) * n_grid_dims,
        vmem_limit_bytes=VMEM_LIMIT_BYTES)


def _resident(shape):
    zeros = (0,) * len(shape)
    return pl.BlockSpec(shape, lambda *_: zeros, pipeline_mode=pl.Buffered(1))


def _rows(tm, width):
    return pl.BlockSpec((tm, width), lambda i: (i, 0))


def _dot(a, b):
    return jnp.dot(a, b, preferred_element_type=F32)


def _rmsnorm(x, g):
    return x * lax.rsqrt(jnp.mean(x * x, axis=-1, keepdims=True) + EPS) * g


def _silu(z):
    return z * jax.nn.sigmoid(z)


def _log_sigmoid(x):
    return jnp.minimum(x, 0.0) - jnp.log1p(jnp.exp(-jnp.abs(x)))


def _ple_residual(x1, p_ref, wg_ref, wp_ref):
    gate = jax.nn.sigmoid(_dot(x1.astype(BF16), wg_ref[...]))
    proj = _dot(p_ref[...].astype(BF16), wp_ref[...])
    return x1 + gate * proj


def _seq_index(idx, seq_len):
    if seq_len & (seq_len - 1) == 0:
        shift = seq_len.bit_length() - 1
        return lax.shift_right_logical(idx, shift), idx & (seq_len - 1)
    return idx // seq_len, idx % seq_len


def _split_bf16(c):
    hi = c.astype(BF16)
    r1 = c - hi.astype(F32)
    mid = r1.astype(BF16)
    lo = (r1 - mid.astype(F32)).astype(BF16)
    return hi, mid, lo


def _bias_pieces_t(ct):
    tm = ct.shape[1]
    ones_row = jnp.where(lax.broadcasted_iota(jnp.int32, (BF16_ROWS, tm), 0) == 0, 1.0, 0.0).astype(BF16)
    return jnp.concatenate(list(_split_bf16(ct)) + [ones_row], axis=0)


def _prefix_sum_lanes(x):
    n = x.shape[1]
    lane = lax.broadcasted_iota(jnp.int32, x.shape, 1)
    shift = 1
    while shift < n:
        x = x + jnp.where(lane >= shift, pltpu.roll(x, shift, axis=1), 0.0)
        shift *= 2
    return x


def _conv_core(x_ref, p_ref, g_ref, win_ref, cw_ref, wout_ref, wg_ref, wp_ref, xo_ref, ubuf,
               fix_first_rows):
    tm, d = x_ref.shape
    x = x_ref[...]
    h = _rmsnorm(x, g_ref[...]).astype(BF16)
    proj = _dot(h, win_ref[...])
    b_gate, c_gate, xin, z = (proj[:, k * d:(k + 1) * d] for k in range(4))
    u = c_gate * xin
    ubuf[SUBLANES:SUBLANES + tm, :] = u
    u1 = ubuf[SUBLANES - 1:SUBLANES - 1 + tm, :]
    u2 = ubuf[SUBLANES - 2:SUBLANES - 2 + tm, :]
    u1, u2 = fix_first_rows(u1, u2)
    cw = cw_ref[...]
    conv = cw[0:1, :] * u2 + cw[1:2, :] * u1 + cw[2:3, :] * u
    y = b_gate * conv * _silu(z)
    x1 = x + _dot(y.astype(BF16), wout_ref[...])
    xo_ref[...] = _ple_residual(x1, p_ref, wg_ref, wp_ref)
    return u


def _conv_long_kernel(x_ref, p_ref, g_ref, win_ref, cw_ref, wout_ref, wg_ref, wp_ref,
                      xo_ref, tail_ref, ubuf, *, tiles_per_seq):
    tm, d = x_ref.shape
    first = (pl.program_id(0) % tiles_per_seq) == 0

    @pl.when(first)
    def _():
        ubuf[0:SUBLANES, :] = jnp.zeros((SUBLANES, d), F32)

    @pl.when(jnp.logical_not(first))
    def _():
        ubuf[0:SUBLANES, :] = ubuf[tm:tm + SUBLANES, :]

    _conv_core(x_ref, p_ref, g_ref, win_ref, cw_ref, wout_ref, wg_ref, wp_ref, xo_ref, ubuf,
               lambda u1, u2: (u1, u2))
    tail_ref[0] = ubuf[tm:tm + SUBLANES, :]


def _conv_short_kernel(x_ref, p_ref, s1_ref, s2_ref, g_ref, win_ref, cw_ref, wout_ref, wg_ref,
                       wp_ref, xo_ref, u_ref, ubuf, *, seq_len):
    tm, d = x_ref.shape
    ubuf[0:SUBLANES, :] = jnp.zeros((SUBLANES, d), F32)
    _, t = _seq_index(lax.broadcasted_iota(jnp.int32, (tm, d), 0), seq_len)

    def fix(u1, u2):
        return (jnp.where(t < 1, s1_ref[...], u1), jnp.where(t < 2, s2_ref[...], u2))

    u_ref[...] = _conv_core(x_ref, p_ref, g_ref, win_ref, cw_ref, wout_ref, wg_ref, wp_ref,
                            xo_ref, ubuf, fix)


def _conv_layer(x, p, state, w, seq_len):
    n, d = x.shape
    dp = p.shape[1]
    n_seq = n // seq_len
    weights = (w["g"], w["w_in"], w["conv_w"], w["w_out"], w["w_gate"], w["w_proj"])
    w_specs = [_resident(a.shape) for a in weights]
    if state is None:
        tm = min(CONV_ROW_TILE, seq_len)
        assert seq_len % tm == 0 and tm % SUBLANES == 0
        x_new, tails = pl.pallas_call(
            functools.partial(_conv_long_kernel, tiles_per_seq=seq_len // tm),
            grid=(n // tm,),
            in_specs=[_rows(tm, d), _rows(tm, dp)] + w_specs,
            out_specs=[_rows(tm, d), pl.BlockSpec((1, SUBLANES, d), lambda i: (i, 0, 0))],
            out_shape=[jax.ShapeDtypeStruct((n, d), F32),
                       jax.ShapeDtypeStruct((n // tm, SUBLANES, d), F32)],
            scratch_shapes=[pltpu.VMEM((tm + SUBLANES, d), F32)],
            compiler_params=_params(1),
            name="conv_layer_prompt",
        )(x, p, *weights)
        last = tails.reshape(n_seq, seq_len // tm, SUBLANES, d)[:, -1]
        return x_new, last[:, SUBLANES - (CONV_W - 1):, :]
    tm = n
    assert tm % seq_len == 0 and seq_len >= CONV_W - 1
    zeros = jnp.zeros((n_seq, seq_len, d), F32)
    s1 = zeros.at[:, 0].set(state[:, 1]).reshape(n, d)
    s2 = zeros.at[:, 0].set(state[:, 0]).at[:, 1].set(state[:, 1]).reshape(n, d)
    x_new, u = pl.pallas_call(
        functools.partial(_conv_short_kernel, seq_len=seq_len),
        grid=(n // tm,),
        in_specs=[_rows(tm, d), _rows(tm, dp), _rows(tm, d), _rows(tm, d)] + w_specs,
        out_specs=[_rows(tm, d), _rows(tm, d)],
        out_shape=[jax.ShapeDtypeStruct((n, d), F32), jax.ShapeDtypeStruct((n, d), F32)],
        scratch_shapes=[pltpu.VMEM((tm + SUBLANES, d), F32)],
        compiler_params=_params(1),
        name="conv_layer_sample",
    )(x, p, s1, s2, *weights)
    return x_new, u.reshape(n_seq, seq_len, d)[:, seq_len - (CONV_W - 1):, :]


def _cumsum_rows(vals, seq_len):
    tm = vals.shape[0]
    r = lax.broadcasted_iota(jnp.int32, (tm, tm), 0)
    c = lax.broadcasted_iota(jnp.int32, (tm, tm), 1)
    if seq_len < tm:
        lower = jnp.where(c <= r, _seq_index(c, seq_len)[0], -1)
        tri = jnp.where(lower == _seq_index(r, seq_len)[0], 1.0, 0.0)
    else:
        tri = jnp.where(c <= r, 1.0, 0.0)
    return jnp.dot(tri.astype(F32), vals, preferred_element_type=F32, precision=_HIGHEST)


def _dot_nt(a, b):
    return lax.dot_general(a, b, (((1,), (1,)), ((), ())), preferred_element_type=F32)


def _kv_long_kernel(x_ref, g_ref, wka_ref, wkt_ref, wvt_ref, wft_ref, bfc_ref,
                    kt_ref, vt_ref, logft_ref, ct_ref, ka_ref, vtb_ref, carry, *, tiles_per_seq):
    tm = x_ref.shape[0]

    @pl.when((pl.program_id(0) % tiles_per_seq) == 0)
    def _():
        carry[...] = jnp.zeros(carry.shape, F32)

    h = _rmsnorm(x_ref[...], g_ref[...]).astype(BF16)
    logf_t = _log_sigmoid(_dot_nt(wft_ref[...], h) + bfc_ref[...])
    logft_ref[0] = logf_t
    run = carry[:, 0:1]
    blocks = []
    for b in range(tm // LANES):
        blocks.append(_prefix_sum_lanes(logf_t[:, b * LANES:(b + 1) * LANES]) + run)
        run = blocks[-1][:, LANES - 1:LANES]
    carry[...] = jnp.broadcast_to(run, carry.shape)
    c2_t = jnp.concatenate(blocks, axis=1) * LOG2_E
    ct_ref[0] = c2_t
    pieces_t = _bias_pieces_t(c2_t).astype(F32)
    pieces = jnp.concatenate([pieces_t, jnp.zeros((LANES - pieces_t.shape[0], tm), F32)], axis=0).T
    lhs = jnp.concatenate([h, pieces.astype(BF16)], axis=1)
    ka_ref[...] = _dot(lhs, wka_ref[...]).astype(BF16)
    kt_ref[0] = _dot_nt(wkt_ref[...], h)
    vt = _dot_nt(wvt_ref[...], h)
    vt_ref[0] = vt
    ones_row = jnp.where(lax.broadcasted_iota(jnp.int32, (BF16_ROWS, tm), 0) == 0, 1.0, 0.0).astype(BF16)
    for hd in range(N_HEADS):
        vtb_ref[0, hd * V_ROWS:hd * V_ROWS + HEAD_DIM, :] = (
            vt[hd * HEAD_DIM:(hd + 1) * HEAD_DIM, :].astype(BF16))
        vtb_ref[0, hd * V_ROWS + HEAD_DIM:(hd + 1) * V_ROWS, :] = ones_row


def _kv_short_kernel(x_ref, g_ref, wa_ref, bf_ref, k_ref, v_ref, logf_ref, c_ref, *, seq_len):
    h = _rmsnorm(x_ref[...], g_ref[...]).astype(BF16)
    r = _dot(h, wa_ref[...])
    k_ref[...] = r[:, :D_ATT]
    v_ref[...] = r[:, D_ATT:2 * D_ATT]
    logf = _log_sigmoid(r[:, 2 * D_ATT:] + bf_ref[...])
    logf_ref[...] = logf[:, :N_HEADS]
    c_ref[...] = _cumsum_rows(logf, seq_len)


def _shared_kv(x, w, n_seq, seq_len, long_seq):
    n, d = x.shape
    if long_seq:
        tm = min(ROW_TILE, seq_len)
        assert seq_len % tm == 0
        tps = seq_len // tm
        weights = (w["g_kv"], w["w_k_aug"], w["w_kt"], w["w_vt"], w["w_ft"], w["b_f_col"])
        by_seq = lambda rows: pl.BlockSpec((1, rows, tm), lambda i: (i // tps, 0, i % tps))
        kt, vt, logft, ct, ka, vtb = pl.pallas_call(
            functools.partial(_kv_long_kernel, tiles_per_seq=tps),
            grid=(n // tm,),
            in_specs=[_rows(tm, d)] + [_resident(a.shape) for a in weights],
            out_specs=[by_seq(D_ATT), by_seq(D_ATT), by_seq(N_HEADS), by_seq(N_HEADS),
                       _rows(tm, N_HEADS * HEAD_PAD), by_seq(N_HEADS * V_ROWS)],
            out_shape=[jax.ShapeDtypeStruct((n_seq, D_ATT, seq_len), F32),
                       jax.ShapeDtypeStruct((n_seq, D_ATT, seq_len), F32),
                       jax.ShapeDtypeStruct((n_seq, N_HEADS, seq_len), F32),
                       jax.ShapeDtypeStruct((n_seq, N_HEADS, seq_len), F32),
                       jax.ShapeDtypeStruct((n, N_HEADS * HEAD_PAD), BF16),
                       jax.ShapeDtypeStruct((n_seq, N_HEADS * V_ROWS, seq_len), BF16)],
            scratch_shapes=[pltpu.VMEM((N_HEADS, LANES), F32)],
            compiler_params=_params(1),
            name="shared_kv_prompt",
        )(x, *weights)
        to_std = lambda a: a.reshape(n_seq, N_HEADS, HEAD_DIM, seq_len).transpose(0, 3, 1, 2)
        return dict(k=to_std(kt), v=to_std(vt), logf=logft.transpose(0, 2, 1), c=ct, ka=ka, vtb=vtb)
    tm = n
    assert tm % seq_len == 0
    weights = (w["g_kv"], w["w_kvf"], w["b_f"])
    k, v, logf, c = pl.pallas_call(
        functools.partial(_kv_short_kernel, seq_len=seq_len),
        grid=(n // tm,),
        in_specs=[_rows(tm, d)] + [_resident(a.shape) for a in weights],
        out_specs=[_rows(tm, D_ATT), _rows(tm, D_ATT), _rows(tm, N_HEADS), _rows(tm, LANES)],
        out_shape=[jax.ShapeDtypeStruct((n, D_ATT), F32), jax.ShapeDtypeStruct((n, D_ATT), F32),
                   jax.ShapeDtypeStruct((n, N_HEADS), F32), jax.ShapeDtypeStruct((n, LANES), F32)],
        compiler_params=_params(1),
        name="shared_kv_sample",
    )(x, *weights)
    std = lambda a: a.reshape(n_seq, seq_len, N_HEADS, HEAD_DIM)
    return dict(k=std(k), v=std(v), logf=logf.reshape(n_seq, seq_len, N_HEADS), c=c, k2=k, v2=v)


def _attn_in_prompt_kernel(x_ref, ct_ref, g_ref, wqt_ref, wz_ref, place_ref, qt_ref, sz_ref):
    tm = x_ref.shape[0]
    h = _rmsnorm(x_ref[...], g_ref[...]).astype(BF16)
    q_t = _dot_nt(wqt_ref[...], h)
    bias_rows = _dot(place_ref[...], _bias_pieces_t(ct_ref[0])).astype(BF16)
    no_rows = jnp.zeros((HEAD_PAD - HEAD_DIM - BF16_ROWS, tm), BF16)
    for hd in range(N_HEADS):
        r0 = hd * HEAD_PAD
        qt_ref[0, r0:r0 + HEAD_DIM, :] = q_t[hd * HEAD_DIM:(hd + 1) * HEAD_DIM, :].astype(BF16)
        qt_ref[0, r0 + HEAD_DIM:r0 + HEAD_DIM + BF16_ROWS, :] = bias_rows[hd * BF16_ROWS:(hd + 1) * BF16_ROWS, :]
        qt_ref[0, r0 + HEAD_DIM + BF16_ROWS:r0 + HEAD_PAD, :] = no_rows
    sz_ref[...] = _silu(_dot(h, wz_ref[...]))


def _attn_in_sample_kernel(x_ref, g_ref, wq_ref, wz_ref, q_ref, sz_ref):
    h = _rmsnorm(x_ref[...], g_ref[...]).astype(BF16)
    q_ref[...] = _dot(h, wq_ref[...])
    sz_ref[...] = _silu(_dot(h, wz_ref[...]))


def _attn_in(x, c, w, n_seq, seq_len, long_seq):
    n, d = x.shape
    if long_seq:
        tm = min(ROW_TILE, seq_len)
        assert seq_len % tm == 0
        tps = seq_len // tm
        weights = (w["g"], w["w_qt"], w["w_z"], w["q_bias_place"])
        return pl.pallas_call(
            _attn_in_prompt_kernel,
            grid=(n // tm,),
            in_specs=[_rows(tm, d), pl.BlockSpec((1, N_HEADS, tm), lambda i: (i // tps, 0, i % tps))]
            + [_resident(a.shape) for a in weights],
            out_specs=[pl.BlockSpec((1, N_HEADS * HEAD_PAD, tm), lambda i: (i // tps, 0, i % tps)),
                       _rows(tm, D_ATT)],
            out_shape=[jax.ShapeDtypeStruct((n_seq, N_HEADS * HEAD_PAD, seq_len), BF16),
                       jax.ShapeDtypeStruct((n, D_ATT), F32)],
            compiler_params=_params(1),
            name="attn_in_prompt",
        )(x, c, *weights)
    tm = min(ROW_TILE, n)
    weights = (w["g"], w["w_q"], w["w_z"])
    return pl.pallas_call(
        _attn_in_sample_kernel,
        grid=(n // tm,),
        in_specs=[_rows(tm, d)] + [_resident(a.shape) for a in weights],
        out_specs=[_rows(tm, D_ATT), _rows(tm, D_ATT)],
        out_shape=[jax.ShapeDtypeStruct((n, D_ATT), F32), jax.ShapeDtypeStruct((n, D_ATT), F32)],
        compiler_params=_params(1),
        name="attn_in_sample",
    )(x, *weights)


def _attn_out_kernel(x_ref, o_ref, sz_ref, p_ref, wout_ref, wg_ref, wp_ref, gf_ref, out_ref, *,
                     final):
    y = (o_ref[...] * sz_ref[...]).astype(BF16)
    x1 = x_ref[...] + _dot(y, wout_ref[...])
    x_new = _ple_residual(x1, p_ref, wg_ref, wp_ref)
    out_ref[...] = _rmsnorm(x_new, gf_ref[...]) if final else x_new


def _attn_out(x, o, sz, p, w, g_final, final):
    n, d = x.shape
    tm = min(ROW_TILE, n)
    weights = (w["w_out"], w["w_gate"], w["w_proj"], g_final)
    return pl.pallas_call(
        functools.partial(_attn_out_kernel, final=final),
        grid=(n // tm,),
        in_specs=[_rows(tm, d), _rows(tm, D_ATT), _rows(tm, D_ATT), _rows(tm, p.shape[1])]
        + [_resident(a.shape) for a in weights],
        out_specs=_rows(tm, d),
        out_shape=jax.ShapeDtypeStruct((n, d), F32),
        compiler_params=_params(1),
        name="attn_out",
    )(x, o, sz, p, *weights)


def _flash_kernel(qt_ref, ka_ref, vt_ref, o_ref, s_scr, mt_scr, m_scr, acc_scr, *, tile):
    i = pl.program_id(2)
    qts = [qt_ref[0, hh * HEAD_PAD:(hh + 1) * HEAD_PAD, :] for hh in range(2)]

    def logits(j, slot, mask):
        off = pl.multiple_of(j * tile, tile)
        for hh in range(2):
            s = _dot(ka_ref[0, pl.ds(off, tile), hh * HEAD_PAD:(hh + 1) * HEAD_PAD], qts[hh])
            if mask is not None:
                s = jnp.where(mask, s, NEG_INF)
            s_scr[slot, hh] = s
            mt_scr[slot, hh] = jnp.max(s, axis=0, keepdims=True)

    def fold(j, slot):
        off = pl.multiple_of(j * tile, tile)
        for hh in range(2):
            m = m_scr[hh]
            m_new = jnp.maximum(m, mt_scr[slot, hh])
            p = jnp.exp2(s_scr[slot, hh] - m_new).astype(BF16)
            vt = vt_ref[0, hh * V_ROWS:(hh + 1) * V_ROWS, pl.ds(off, tile)]
            acc_scr[hh] = jnp.exp2(m - m_new) * acc_scr[hh] + _dot(vt, p)
            m_scr[hh] = m_new

    m_scr[...] = jnp.full(m_scr.shape, NEG_INF, F32)
    acc_scr[...] = jnp.zeros(acc_scr.shape, F32)
    key = lax.broadcasted_iota(jnp.int32, (tile, tile), 0)
    qry = lax.broadcasted_iota(jnp.int32, (tile, tile), 1)
    logits(i, 0, key <= qry)
    n_tiles = i + 1
    visit = lambda k: jnp.where(k == 0, i, k - 1)

    def two_steps(u, _):
        k = 2 * u
        logits(visit(k + 1), 1, None)
        fold(visit(k), 0)
        logits(visit(k + 2), 0, None)
        fold(visit(k + 1), 1)
        return 0

    n_pairs = (n_tiles - 1) // 2
    lax.fori_loop(0, n_pairs, two_steps, 0)
    k = 2 * n_pairs

    @pl.when(n_tiles - k == 2)
    def _():
        logits(visit(k + 1), 1, None)
        fold(visit(k), 0)
        fold(visit(k + 1), 1)

    @pl.when(n_tiles - k == 1)
    def _():
        fold(visit(k), 0)

    o_t = jnp.concatenate(
        [acc_scr[hh, :HEAD_DIM, :] / acc_scr[hh, HEAD_DIM:HEAD_DIM + 1, :] for hh in range(2)], axis=0)
    o_ref[0] = o_t.T


def _prompt_attention(qt, ka, vtb, n_seq, seq_len):
    tile = min(Q_TILE, seq_len)
    assert seq_len % tile == 0
    pair = 2 * HEAD_PAD
    ka3 = ka.reshape(n_seq, seq_len, N_HEADS * HEAD_PAD)
    o = pl.pallas_call(
        functools.partial(_flash_kernel, tile=tile),
        grid=(n_seq, N_HEADS // 2, seq_len // tile),
        in_specs=[pl.BlockSpec((1, pair, tile), lambda b, hp, i: (b, hp, i)),
                  pl.BlockSpec((1, seq_len, pair), lambda b, hp, i: (b, 0, hp)),
                  pl.BlockSpec((1, 2 * V_ROWS, seq_len), lambda b, hp, i: (b, hp, 0))],
        out_specs=pl.BlockSpec((1, tile, 2 * HEAD_DIM), lambda b, hp, i: (b, i, hp)),
        out_shape=jax.ShapeDtypeStruct((n_seq, seq_len, D_ATT), F32),
        scratch_shapes=[pltpu.VMEM((2, 2, tile, tile), F32),
                        pltpu.VMEM((2, 2, 1, tile), F32),
                        pltpu.VMEM((2, 1, tile), F32),
                        pltpu.VMEM((2, V_ROWS, tile), F32)],
        compiler_params=_params(3),
        name="prompt_attention",
    )(qt, ka3, vtb)
    return o.reshape(n_seq * seq_len, D_ATT)


def _online_update(s, pv_of, m_s, l_s, acc_s):
    m_prev = m_s[...]
    m_new = jnp.maximum(m_prev, jnp.max(s, axis=-1, keepdims=True))
    alpha = jnp.exp(m_prev - m_new)
    p = jnp.exp(s - m_new)
    l_s[...] = alpha * l_s[...] + jnp.sum(p, axis=-1, keepdims=True)
    acc_s[...] = alpha * acc_s[...] + pv_of(p.astype(BF16))
    m_s[...] = m_new


def _suffix_sum_lanes(x):
    n = x.shape[1]
    lane = lax.broadcasted_iota(jnp.int32, x.shape, 1)
    y = jnp.where(lane + 1 < n, pltpu.roll(x, n - 1, axis=1), 0.0)
    shift = 1
    while shift < n:
        y = y + jnp.where(lane + shift < n, pltpu.roll(y, n - shift, axis=1), 0.0)
        shift *= 2
    return y


def _per_head_rows(bias, seq_len):
    return jnp.concatenate(
        [jnp.broadcast_to(bias[h:h + 1, :], (seq_len, bias.shape[1])) for h in range(N_HEADS)], axis=0)


def _sample_attn_kernel(pt_ref, q_ref, cnq_ref, kn_ref, vn_ref, cnk_ref, *refs, ppc):
    del pt_ref
    k_refs, v_refs, f_refs = refs[:ppc], refs[ppc:2 * ppc], refs[2 * ppc:3 * ppc]
    o_ref, qbd_s, m_s, l_s, acc_s, tot_s = refs[3 * ppc:]
    seq_len = q_ref.shape[0]
    rows = N_HEADS * seq_len
    step = pl.program_id(1)
    row_head = _seq_index(lax.broadcasted_iota(jnp.int32, (rows, D_ATT), 0), seq_len)[0]
    own_head = row_head == _seq_index(lax.broadcasted_iota(jnp.int32, (rows, D_ATT), 1), HEAD_DIM)[0]

    @pl.when(step == 0)
    def _():
        q_rows = jnp.concatenate([q_ref[...]] * N_HEADS, axis=0)
        qbd_s[...] = jnp.where(own_head, q_rows, 0.0).astype(BF16)
        m_s[...] = jnp.full(m_s.shape, NEG_INF, F32)
        l_s[...] = jnp.zeros(l_s.shape, F32)
        acc_s[...] = jnp.zeros(acc_s.shape, F32)
        tot_s[...] = jnp.zeros(tot_s.shape, F32)

    page = k_refs[0].shape[3]
    kc = jnp.concatenate([r[0].reshape(D_ATT, page) for r in k_refs], axis=1).astype(BF16)
    vc = jnp.concatenate([r[0].reshape(D_ATT, page) for r in v_refs], axis=1).astype(BF16)

    after = tot_s[...]
    rest_pages = [None] * ppc
    for pg in reversed(range(ppc)):
        logf_pg = f_refs[pg][0]
        rest_pages[pg] = _suffix_sum_lanes(logf_pg) + after
        after = after + jnp.sum(logf_pg, axis=1, keepdims=True)
    tot_s[...] = after
    rest = jnp.concatenate(rest_pages, axis=1)

    s = _dot(qbd_s[...], kc) + _per_head_rows(rest, seq_len) + cnq_ref[0]
    _online_update(s, lambda p: _dot_nt(p, vc), m_s, l_s, acc_s)

    @pl.when(step == pl.num_programs(1) - 1)
    def _():
        no_keys = jnp.zeros((LANES - seq_len, D_ATT), F32)
        kn = jnp.concatenate([kn_ref[...], no_keys], axis=0).astype(BF16)
        vn = jnp.concatenate([vn_ref[...], no_keys], axis=0).astype(BF16)
        s2 = _dot_nt(qbd_s[...], kn) - _per_head_rows(cnk_ref[0], seq_len) + cnq_ref[0]
        t = _seq_index(lax.broadcasted_iota(jnp.int32, s2.shape, 0), seq_len)[1]
        sp = lax.broadcasted_iota(jnp.int32, s2.shape, 1)
        s2 = jnp.where(sp <= t, s2, NEG_INF)
        _online_update(s2, lambda p: _dot(p, vn), m_s, l_s, acc_s)
        o_blocks = jnp.where(own_head, acc_s[...] / l_s[...], 0.0)
        o_ref[...] = functools.reduce(
            lambda a, b: a + b, [o_blocks[h * seq_len:(h + 1) * seq_len, :] for h in range(N_HEADS)])


def _sample_attention(q, k_new, v_new, c_new, cache_k, cache_v, cache_logf, page_table, seq_len):
    n = q.shape[0]
    n_seq = n // seq_len
    n_pages = page_table.shape[1]
    page = cache_k.shape[1]
    ppc = PAGES_PER_STEP
    assert n_pages % ppc == 0 and seq_len <= LANES
    n_chunks = n_pages // ppc

    cache_kt = cache_k.transpose(0, 2, 3, 1)
    cache_vt = cache_v.transpose(0, 2, 3, 1)
    cache_ft = cache_logf.transpose(0, 2, 1)

    rows = N_HEADS * seq_len
    cn = c_new[:, :N_HEADS].reshape(n_seq, seq_len, N_HEADS).transpose(0, 2, 1)
    cnq = cn.reshape(n_seq, rows, 1)
    cnk = jnp.pad(cn, ((0, 0), (0, 0), (0, LANES - seq_len)))

    def page_map(j):
        return lambda b, c, pt: (pt[b, (n_chunks - 1 - c) * ppc + j], 0, 0, 0)

    def logf_map(j):
        return lambda b, c, pt: (pt[b, (n_chunks - 1 - c) * ppc + j], 0, 0)

    seq_rows = pl.BlockSpec((seq_len, D_ATT), lambda b, c, pt: (b, 0))
    kv_page = lambda j: pl.BlockSpec((1, N_HEADS, HEAD_DIM, page), page_map(j))
    return pl.pallas_call(
        functools.partial(_sample_attn_kernel, ppc=ppc),
        grid_spec=pltpu.PrefetchScalarGridSpec(
            num_scalar_prefetch=1,
            grid=(n_seq, n_chunks),
            in_specs=[seq_rows, pl.BlockSpec((1, rows, 1), lambda b, c, pt: (b, 0, 0)),
                      seq_rows, seq_rows,
                      pl.BlockSpec((1, N_HEADS, LANES), lambda b, c, pt: (b, 0, 0))]
            + [kv_page(j) for j in range(ppc)] + [kv_page(j) for j in range(ppc)]
            + [pl.BlockSpec((1, N_HEADS, page), logf_map(j)) for j in range(ppc)],
            out_specs=seq_rows,
            scratch_shapes=[pltpu.VMEM((rows, D_ATT), BF16),
                            pltpu.VMEM((rows, 1), F32), pltpu.VMEM((rows, 1), F32),
                            pltpu.VMEM((rows, D_ATT), F32),
                            pltpu.VMEM((N_HEADS, 1), F32)]),
        out_shape=jax.ShapeDtypeStruct((n, D_ATT), F32),
        compiler_params=_params(2),
        name="sample_attention",
    )(page_table, q, cnq, k_new, v_new,
      cnk, *([cache_kt] * ppc), *([cache_vt] * ppc), *([cache_ft] * ppc))


def _pad_heads(w):
    d = w.shape[0]
    w = w.reshape(d, N_HEADS, HEAD_DIM)
    return jnp.pad(w, ((0, 0), (0, 0), (0, HEAD_PAD - HEAD_DIM))).reshape(d, N_HEADS * HEAD_PAD)


def _k_bias_placement():
    rows = np.zeros((LANES, N_HEADS * HEAD_PAD), np.float32)
    for h in range(N_HEADS):
        for part in range(N_PARTS):
            rows[part * N_HEADS + h, h * HEAD_PAD + HEAD_DIM + part] = -1.0
            rows[N_PARTS * N_HEADS, h * HEAD_PAD + HEAD_DIM + N_PARTS + part] = 1.0
    return jnp.asarray(rows, BF16)


def _q_bias_placement():
    place = np.zeros((N_HEADS * BF16_ROWS, N_PARTS * N_HEADS + BF16_ROWS), np.float32)
    for h in range(N_HEADS):
        for part in range(N_PARTS):
            place[h * BF16_ROWS + part, N_PARTS * N_HEADS] = 1.0
            place[h * BF16_ROWS + N_PARTS + part, part * N_HEADS + h] = 1.0
    return jnp.asarray(place, BF16)


def _prepare_weights(norm_a, w_in_a, conv_w_a, w_out_a, norm_kv, w_kv, b_f, norm_b, w_in_b, w_out_b,
                     w_ple_proj, w_ple_gate, norm_f):
    n_a, n_b = w_in_a.shape[0], w_in_b.shape[0]
    scale = HEAD_DIM ** -0.5
    k_rows = _k_bias_placement()
    q_place = _q_bias_placement()
    ple = lambda i: dict(w_gate=w_ple_gate[i].astype(BF16), w_proj=w_ple_proj[i].astype(BF16))
    conv = [dict(g=norm_a[i][None, :], w_in=w_in_a[i].astype(BF16), conv_w=conv_w_a[i],
                 w_out=w_out_a[i].astype(BF16), **ple(i)) for i in range(n_a)]
    w_k, w_v, w_f = w_kv[:, :D_ATT], w_kv[:, D_ATT:2 * D_ATT], w_kv[:, 2 * D_ATT:]
    w_f_pad = jnp.pad(w_f, ((0, 0), (0, LANES - N_HEADS)))
    kv = dict(g_kv=norm_kv[None, :],
              w_kvf=jnp.concatenate([w_k, w_v, w_f_pad], axis=1).astype(BF16),
              b_f=jnp.pad(b_f, (0, LANES - N_HEADS))[None, :],
              b_f_col=b_f[:, None],
              w_k_aug=jnp.concatenate([_pad_heads(w_k).astype(BF16), k_rows], axis=0),
              w_kt=w_k.T.astype(BF16), w_vt=w_v.T.astype(BF16), w_ft=w_f.T.astype(BF16))
    attn = []
    for j in range(n_b):
        w_q = w_in_b[j][:, :D_ATT] * scale
        attn.append(dict(g=norm_b[j][None, :], w_q=w_q.astype(BF16),
                         w_qt=(w_q * LOG2_E).T.astype(BF16), q_bias_place=q_place,
                         w_z=w_in_b[j][:, D_ATT:].astype(BF16),
                         w_out=w_out_b[j].astype(BF16), **ple(n_a + j)))
    return conv, kv, attn, norm_f[None, :]


def _trunk(x3, p4, conv_state, weights, attend):
    conv_w, kv_w, attn_w, g_final = weights
    n_seq, seq_len, d = x3.shape
    n = n_seq * seq_len
    long_seq = conv_state is None
    x = x3.reshape(n, d)
    p = p4.reshape(p4.shape[0], n, p4.shape[-1])
    new_conv = []
    for i, w in enumerate(conv_w):
        x, st = _conv_layer(x, p[i], None if long_seq else conv_state[i], w, seq_len)
        new_conv.append(st)
    kv = _shared_kv(x, kv_w, n_seq, seq_len, long_seq)
    for j, w in enumerate(attn_w):
        q, sz = _attn_in(x, kv["c"], w, n_seq, seq_len, long_seq)
        o = attend(q, kv)
        x = _attn_out(x, o, sz, p[len(conv_w) + j], w, g_final, final=(j == len(attn_w) - 1))
    return x.reshape(n_seq, seq_len, d), jnp.stack(new_conv), kv["k"], kv["v"], kv["logf"]


def kernel(x_prompt, x_sample, p_prompt, p_sample, state_conv, cache_k, cache_v, cache_logf, page_table, norm_a, w_in_a, conv_w_a, w_out_a, norm_kv, w_kv, b_f, norm_b, w_in_b, w_out_b, w_ple_proj, w_ple_gate, norm_f):
    assert conv_w_a.shape[1] == CONV_W and w_kv.shape[1] == 2 * D_ATT + N_HEADS
    weights = _prepare_weights(norm_a, w_in_a, conv_w_a, w_out_a, norm_kv, w_kv, b_f, norm_b, w_in_b,
                               w_out_b, w_ple_proj, w_ple_gate, norm_f)
    n_prompt, prompt_len = x_prompt.shape[:2]
    sample_len = x_sample.shape[1]

    def prompt_attend(qt, kv):
        return _prompt_attention(qt, kv["ka"], kv["vtb"], n_prompt, prompt_len)

    def sample_attend(q, kv):
        return _sample_attention(q, kv["k2"], kv["v2"], kv["c"], cache_k, cache_v, cache_logf,
                                 page_table, sample_len)

    y_p, conv_p, k_p, v_p, logf_p = _trunk(x_prompt, p_prompt, None, weights, prompt_attend)
    y_s, conv_s, k_s, v_s, logf_s = _trunk(x_sample, p_sample, state_conv, weights, sample_attend)
    return (y_p, y_s, conv_p, k_p, v_p, logf_p, conv_s, k_s, v_s, logf_s)
```

```python
import functools

import numpy as np
import jax
import jax.numpy as jnp
from jax import lax
from jax.experimental import pallas as pl
from jax.experimental.pallas import tpu as pltpu

F32 = jnp.float32
BF16 = jnp.bfloat16

N_HEADS = 16
HEAD_DIM = 64
D_ATT = N_HEADS * HEAD_DIM
CONV_W = 3
EPS = 1e-6
NEG_INF = -1e30
LOG2_E = 1.4426950408889634

LANES = 128
SUBLANES = 8
HEAD_PAD = LANES
N_PARTS = 3
BF16_ROWS = 16
V_ROWS = HEAD_DIM + BF16_ROWS
VMEM_LIMIT_BYTES = 56 * 2**20

ROW_TILE = 1024
CONV_ROW_TILE = 256
Q_TILE = 512
FLASH_SEQS = 2
PAGES_PER_STEP = 8
_HIGHEST = lax.Precision.HIGHEST


def _params(n_grid_dims):
    return pltpu.CompilerParams(
        dimension_semantics=("arbitrary",) * n_grid_dims,
        vmem_limit_bytes=VMEM_LIMIT_BYTES)


def _resident(shape):
    zeros = (0,) * len(shape)
    return pl.BlockSpec(shape, lambda *_: zeros, pipeline_mode=pl.Buffered(1))


def _rows(tm, width):
    return pl.BlockSpec((tm, width), lambda i: (i, 0))


def _dot(a, b):
    return jnp.dot(a, b, preferred_element_type=F32)


def _rmsnorm(x, g):
    return x * lax.rsqrt(jnp.mean(x * x, axis=-1, keepdims=True) + EPS) * g


def _silu(z):
    return z * jax.nn.sigmoid(z)


def _log_sigmoid(x):
    return jnp.minimum(x, 0.0) - jnp.log1p(jnp.exp(-jnp.abs(x)))


def _ple_residual(x1, p_ref, wg_ref, wp_ref):
    gate = jax.nn.sigmoid(_dot(x1.astype(BF16), wg_ref[...]))
    proj = _dot(p_ref[...].astype(BF16), wp_ref[...])
    return x1 + gate * proj


def _seq_index(idx, seq_len):
    if seq_len & (seq_len - 1) == 0:
        shift = seq_len.bit_length() - 1
        return lax.shift_right_logical(idx, shift), idx & (seq_len - 1)
    return idx // seq_len, idx % seq_len


def _split_bf16(c):
    hi = c.astype(BF16)
    r1 = c - hi.astype(F32)
    mid = r1.astype(BF16)
    lo = (r1 - mid.astype(F32)).astype(BF16)
    return hi, mid, lo


def _bias_pieces_t(ct):
    tm = ct.shape[1]
    ones_row = jnp.where(lax.broadcasted_iota(jnp.int32, (BF16_ROWS, tm), 0) == 0, 1.0, 0.0).astype(BF16)
    return jnp.concatenate(list(_split_bf16(ct)) + [ones_row], axis=0)


def _prefix_sum_lanes(x):
    n = x.shape[1]
    lane = lax.broadcasted_iota(jnp.int32, x.shape, 1)
    shift = 1
    while shift < n:
        x = x + jnp.where(lane >= shift, pltpu.roll(x, shift, axis=1), 0.0)
        shift *= 2
    return x


def _conv_core(x_ref, p_ref, g_ref, win_ref, cw_ref, wout_ref, wg_ref, wp_ref, xo_ref, ubuf,
               fix_first_rows):
    tm, d = x_ref.shape
    x = x_ref[...]
    h = _rmsnorm(x, g_ref[...]).astype(BF16)
    proj = _dot(h, win_ref[...])
    b_gate, c_gate, xin, z = (proj[:, k * d:(k + 1) * d] for k in range(4))
    u = c_gate * xin
    ubuf[SUBLANES:SUBLANES + tm, :] = u
    u1 = ubuf[SUBLANES - 1:SUBLANES - 1 + tm, :]
    u2 = ubuf[SUBLANES - 2:SUBLANES - 2 + tm, :]
    u1, u2 = fix_first_rows(u1, u2)
    cw = cw_ref[...]
    conv = cw[0:1, :] * u2 + cw[1:2, :] * u1 + cw[2:3, :] * u
    y = b_gate * conv * _silu(z)
    x1 = x + _dot(y.astype(BF16), wout_ref[...])
    xo_ref[...] = _ple_residual(x1, p_ref, wg_ref, wp_ref)
    return u


def _conv_long_kernel(x_ref, p_ref, g_ref, win_ref, cw_ref, wout_ref, wg_ref, wp_ref,
                      xo_ref, tail_ref, ubuf, *, tiles_per_seq):
    tm, d = x_ref.shape
    first = (pl.program_id(0) % tiles_per_seq) == 0

    @pl.when(first)
    def _():
        ubuf[0:SUBLANES, :] = jnp.zeros((SUBLANES, d), F32)

    @pl.when(jnp.logical_not(first))
    def _():
        ubuf[0:SUBLANES, :] = ubuf[tm:tm + SUBLANES, :]

    _conv_core(x_ref, p_ref, g_ref, win_ref, cw_ref, wout_ref, wg_ref, wp_ref, xo_ref, ubuf,
               lambda u1, u2: (u1, u2))
    tail_ref[0] = ubuf[tm:tm + SUBLANES, :]


def _conv_short_kernel(x_ref, p_ref, s1_ref, s2_ref, g_ref, win_ref, cw_ref, wout_ref, wg_ref,
                       wp_ref, xo_ref, u_ref, ubuf, *, seq_len):
    tm, d = x_ref.shape
    ubuf[0:SUBLANES, :] = jnp.zeros((SUBLANES, d), F32)
    _, t = _seq_index(lax.broadcasted_iota(jnp.int32, (tm, d), 0), seq_len)

    def fix(u1, u2):
        return (jnp.where(t < 1, s1_ref[...], u1), jnp.where(t < 2, s2_ref[...], u2))

    u_ref[...] = _conv_core(x_ref, p_ref, g_ref, win_ref, cw_ref, wout_ref, wg_ref, wp_ref,
                            xo_ref, ubuf, fix)


def _layer_rows(tm, p_all, layer):
    return pl.BlockSpec((None, tm, p_all.shape[2]), lambda i: (layer, i, 0))


def _conv_layer(x, p_all, layer, state, w, seq_len):
    n, d = x.shape
    n_seq = n // seq_len
    weights = (w["g"], w["w_in"], w["conv_w"], w["w_out"], w["w_gate"], w["w_proj"])
    w_specs = [_resident(a.shape) for a in weights]
    if state is None:
        tm = min(CONV_ROW_TILE, seq_len)
        assert seq_len % tm == 0 and tm % SUBLANES == 0
        x_new, tails = pl.pallas_call(
            functools.partial(_conv_long_kernel, tiles_per_seq=seq_len // tm),
            grid=(n // tm,),
            in_specs=[_rows(tm, d), _layer_rows(tm, p_all, layer)] + w_specs,
            out_specs=[_rows(tm, d), pl.BlockSpec((1, SUBLANES, d), lambda i: (i, 0, 0))],
            out_shape=[jax.ShapeDtypeStruct((n, d), F32),
                       jax.ShapeDtypeStruct((n // tm, SUBLANES, d), F32)],
            scratch_shapes=[pltpu.VMEM((tm + SUBLANES, d), F32)],
            compiler_params=_params(1),
            name="conv_layer_prompt",
        )(x, p_all, *weights)
        last = tails.reshape(n_seq, seq_len // tm, SUBLANES, d)[:, -1]
        return x_new, last[:, SUBLANES - (CONV_W - 1):, :]
    tm = n
    assert tm % seq_len == 0 and seq_len >= CONV_W - 1
    zeros = jnp.zeros((n_seq, seq_len, d), F32)
    s1 = zeros.at[:, 0].set(state[:, 1]).reshape(n, d)
    s2 = zeros.at[:, 0].set(state[:, 0]).at[:, 1].set(state[:, 1]).reshape(n, d)
    x_new, u = pl.pallas_call(
        functools.partial(_conv_short_kernel, seq_len=seq_len),
        grid=(n // tm,),
        in_specs=[_rows(tm, d), _layer_rows(tm, p_all, layer), _rows(tm, d), _rows(tm, d)] + w_specs,
        out_specs=[_rows(tm, d), _rows(tm, d)],
        out_shape=[jax.ShapeDtypeStruct((n, d), F32), jax.ShapeDtypeStruct((n, d), F32)],
        scratch_shapes=[pltpu.VMEM((tm + SUBLANES, d), F32)],
        compiler_params=_params(1),
        name="conv_layer_sample",
    )(x, p_all, s1, s2, *weights)
    return x_new, u.reshape(n_seq, seq_len, d)[:, seq_len - (CONV_W - 1):, :]


def _cumsum_rows(vals, seq_len):
    tm = vals.shape[0]
    r = lax.broadcasted_iota(jnp.int32, (tm, tm), 0)
    c = lax.broadcasted_iota(jnp.int32, (tm, tm), 1)
    if seq_len < tm:
        lower = jnp.where(c <= r, _seq_index(c, seq_len)[0], -1)
        tri = jnp.where(lower == _seq_index(r, seq_len)[0], 1.0, 0.0)
    else:
        tri = jnp.where(c <= r, 1.0, 0.0)
    return jnp.dot(tri.astype(F32), vals, preferred_element_type=F32, precision=_HIGHEST)


def _dot_nt(a, b):
    return lax.dot_general(a, b, (((1,), (1,)), ((), ())), preferred_element_type=F32)


def _kv_long_kernel(x_ref, g_ref, wka_ref, wkt_ref, wvt_ref, wft_ref, bfc_ref,
                    kt_ref, vt_ref, logft_ref, ct_ref, ka_ref, vtb_ref, carry, *, tiles_per_seq):
    tm = x_ref.shape[0]

    @pl.when((pl.program_id(0) % tiles_per_seq) == 0)
    def _():
        carry[...] = jnp.zeros(carry.shape, F32)

    h = _rmsnorm(x_ref[...], g_ref[...]).astype(BF16)
    logf_t = _log_sigmoid(_dot_nt(wft_ref[...], h) + bfc_ref[...])
    logft_ref[0] = logf_t
    run = carry[:, 0:1]
    blocks = []
    for b in range(tm // LANES):
        blocks.append(_prefix_sum_lanes(logf_t[:, b * LANES:(b + 1) * LANES]) + run)
        run = blocks[-1][:, LANES - 1:LANES]
    carry[...] = jnp.broadcast_to(run, carry.shape)
    c2_t = jnp.concatenate(blocks, axis=1) * LOG2_E
    ct_ref[0] = c2_t
    pieces_t = _bias_pieces_t(c2_t).astype(F32)
    pieces = jnp.concatenate([pieces_t, jnp.zeros((LANES - pieces_t.shape[0], tm), F32)], axis=0).T
    lhs = jnp.concatenate([h, pieces.astype(BF16)], axis=1)
    ka_ref[...] = _dot(lhs, wka_ref[...]).astype(BF16)
    kt_ref[0] = _dot_nt(wkt_ref[...], h)
    vt = _dot_nt(wvt_ref[...], h)
    vt_ref[0] = vt
    ones_row = jnp.where(lax.broadcasted_iota(jnp.int32, (BF16_ROWS, tm), 0) == 0, 1.0, 0.0).astype(BF16)
    for hd in range(N_HEADS):
        vtb_ref[0, hd * V_ROWS:hd * V_ROWS + HEAD_DIM, :] = (
            vt[hd * HEAD_DIM:(hd + 1) * HEAD_DIM, :].astype(BF16))
        vtb_ref[0, hd * V_ROWS + HEAD_DIM:(hd + 1) * V_ROWS, :] = ones_row


def _kv_short_kernel(x_ref, g_ref, wa_ref, bf_ref, k_ref, v_ref, logf_ref, c_ref, *, seq_len):
    h = _rmsnorm(x_ref[...], g_ref[...]).astype(BF16)
    r = _dot(h, wa_ref[...])
    k_ref[...] = r[:, :D_ATT]
    v_ref[...] = r[:, D_ATT:2 * D_ATT]
    logf = _log_sigmoid(r[:, 2 * D_ATT:] + bf_ref[...])
    logf_ref[...] = logf[:, :N_HEADS]
    c_ref[...] = _cumsum_rows(logf, seq_len)


def _shared_kv(x, w, n_seq, seq_len, long_seq):
    n, d = x.shape
    if long_seq:
        tm = min(ROW_TILE, seq_len)
        assert seq_len % tm == 0
        tps = seq_len // tm
        weights = (w["g_kv"], w["w_k_aug"], w["w_kt"], w["w_vt"], w["w_ft"], w["b_f_col"])
        by_seq = lambda rows: pl.BlockSpec((1, rows, tm), lambda i: (i // tps, 0, i % tps))
        kt, vt, logft, ct, ka, vtb = pl.pallas_call(
            functools.partial(_kv_long_kernel, tiles_per_seq=tps),
            grid=(n // tm,),
            in_specs=[_rows(tm, d)] + [_resident(a.shape) for a in weights],
            out_specs=[by_seq(D_ATT), by_seq(D_ATT), by_seq(N_HEADS), by_seq(N_HEADS),
                       _rows(tm, N_HEADS * HEAD_PAD), by_seq(N_HEADS * V_ROWS)],
            out_shape=[jax.ShapeDtypeStruct((n_seq, D_ATT, seq_len), F32),
                       jax.ShapeDtypeStruct((n_seq, D_ATT, seq_len), F32),
                       jax.ShapeDtypeStruct((n_seq, N_HEADS, seq_len), F32),
                       jax.ShapeDtypeStruct((n_seq, N_HEADS, seq_len), F32),
                       jax.ShapeDtypeStruct((n, N_HEADS * HEAD_PAD), BF16),
                       jax.ShapeDtypeStruct((n_seq, N_HEADS * V_ROWS, seq_len), BF16)],
            scratch_shapes=[pltpu.VMEM((N_HEADS, LANES), F32)],
            compiler_params=_params(1),
            name="shared_kv_prompt",
        )(x, *weights)
        to_std = lambda a: a.reshape(n_seq, N_HEADS, HEAD_DIM, seq_len).transpose(0, 3, 1, 2)
        return dict(k=to_std(kt), v=to_std(vt), logf=logft.transpose(0, 2, 1), c=ct, ka=ka, vtb=vtb)
    tm = n
    assert tm % seq_len == 0
    weights = (w["g_kv"], w["w_kvf"], w["b_f"])
    k, v, logf, c = pl.pallas_call(
        functools.partial(_kv_short_kernel, seq_len=seq_len),
        grid=(n // tm,),
        in_specs=[_rows(tm, d)] + [_resident(a.shape) for a in weights],
        out_specs=[_rows(tm, D_ATT), _rows(tm, D_ATT), _rows(tm, N_HEADS), _rows(tm, LANES)],
        out_shape=[jax.ShapeDtypeStruct((n, D_ATT), F32), jax.ShapeDtypeStruct((n, D_ATT), F32),
                   jax.ShapeDtypeStruct((n, N_HEADS), F32), jax.ShapeDtypeStruct((n, LANES), F32)],
        compiler_params=_params(1),
        name="shared_kv_sample",
    )(x, *weights)
    std = lambda a: a.reshape(n_seq, seq_len, N_HEADS, HEAD_DIM)
    return dict(k=std(k), v=std(v), logf=logf.reshape(n_seq, seq_len, N_HEADS), c=c, k2=k, v2=v)


def _attn_in_prompt_kernel(x_ref, ct_ref, g_ref, wqt_ref, wz_ref, place_ref, qt_ref, sz_ref):
    tm = x_ref.shape[0]
    h = _rmsnorm(x_ref[...], g_ref[...]).astype(BF16)
    q_t = _dot_nt(wqt_ref[...], h)
    bias_rows = _dot(place_ref[...], _bias_pieces_t(ct_ref[0])).astype(BF16)
    no_rows = jnp.zeros((HEAD_PAD - HEAD_DIM - BF16_ROWS, tm), BF16)
    for hd in range(N_HEADS):
        r0 = hd * HEAD_PAD
        qt_ref[0, r0:r0 + HEAD_DIM, :] = q_t[hd * HEAD_DIM:(hd + 1) * HEAD_DIM, :].astype(BF16)
        qt_ref[0, r0 + HEAD_DIM:r0 + HEAD_DIM + BF16_ROWS, :] = bias_rows[hd * BF16_ROWS:(hd + 1) * BF16_ROWS, :]
        qt_ref[0, r0 + HEAD_DIM + BF16_ROWS:r0 + HEAD_PAD, :] = no_rows
    sz_ref[...] = _silu(_dot(h, wz_ref[...]))


def _attn_in_sample_kernel(x_ref, g_ref, wq_ref, wz_ref, q_ref, sz_ref):
    h = _rmsnorm(x_ref[...], g_ref[...]).astype(BF16)
    q_ref[...] = _dot(h, wq_ref[...])
    sz_ref[...] = _silu(_dot(h, wz_ref[...]))


def _attn_in(x, c, w, n_seq, seq_len, long_seq):
    n, d = x.shape
    if long_seq:
        tm = min(ROW_TILE, seq_len)
        assert seq_len % tm == 0
        tps = seq_len // tm
        weights = (w["g"], w["w_qt"], w["w_z"], w["q_bias_place"])
        return pl.pallas_call(
            _attn_in_prompt_kernel,
            grid=(n // tm,),
            in_specs=[_rows(tm, d), pl.BlockSpec((1, N_HEADS, tm), lambda i: (i // tps, 0, i % tps))]
            + [_resident(a.shape) for a in weights],
            out_specs=[pl.BlockSpec((1, N_HEADS * HEAD_PAD, tm), lambda i: (i // tps, 0, i % tps)),
                       _rows(tm, D_ATT)],
            out_shape=[jax.ShapeDtypeStruct((n_seq, N_HEADS * HEAD_PAD, seq_len), BF16),
                       jax.ShapeDtypeStruct((n, D_ATT), F32)],
            compiler_params=_params(1),
            name="attn_in_prompt",
        )(x, c, *weights)
    tm = min(ROW_TILE, n)
    weights = (w["g"], w["w_q"], w["w_z"])
    return pl.pallas_call(
        _attn_in_sample_kernel,
        grid=(n // tm,),
        in_specs=[_rows(tm, d)] + [_resident(a.shape) for a in weights],
        out_specs=[_rows(tm, D_ATT), _rows(tm, D_ATT)],
        out_shape=[jax.ShapeDtypeStruct((n, D_ATT), F32), jax.ShapeDtypeStruct((n, D_ATT), F32)],
        compiler_params=_params(1),
        name="attn_in_sample",
    )(x, *weights)


def _attn_out_kernel(x_ref, o_ref, sz_ref, p_ref, wout_ref, wg_ref, wp_ref, gf_ref, out_ref, *,
                     final):
    y = (o_ref[...] * sz_ref[...]).astype(BF16)
    x1 = x_ref[...] + _dot(y, wout_ref[...])
    x_new = _ple_residual(x1, p_ref, wg_ref, wp_ref)
    out_ref[...] = _rmsnorm(x_new, gf_ref[...]) if final else x_new


def _attn_out(x, o, sz, p_all, layer, w, g_final, final):
    n, d = x.shape
    tm = min(ROW_TILE, n)
    weights = (w["w_out"], w["w_gate"], w["w_proj"], g_final)
    return pl.pallas_call(
        functools.partial(_attn_out_kernel, final=final),
        grid=(n // tm,),
        in_specs=[_rows(tm, d), _rows(tm, D_ATT), _rows(tm, D_ATT), _layer_rows(tm, p_all, layer)]
        + [_resident(a.shape) for a in weights],
        out_specs=_rows(tm, d),
        out_shape=jax.ShapeDtypeStruct((n, d), F32),
        compiler_params=_params(1),
        name="attn_out",
    )(x, o, sz, p_all, *weights)


def _flash_kernel(qt_ref, ka_ref, vt_ref, o_ref, s_scr, mt_scr, m_scr, acc_scr, *, tile):
    i = pl.program_id(2)
    chains = [(b, hh) for b in range(qt_ref.shape[0]) for hh in range(2)]
    qts = [qt_ref[b, hh * HEAD_PAD:(hh + 1) * HEAD_PAD, :] for b, hh in chains]

    def logits(j, slot, mask):
        off = pl.multiple_of(j * tile, tile)
        for c, (b, hh) in enumerate(chains):
            s = _dot(ka_ref[b, pl.ds(off, tile), hh * HEAD_PAD:(hh + 1) * HEAD_PAD], qts[c])
            if mask is not None:
                s = jnp.where(mask, s, NEG_INF)
            s_scr[slot, c] = s
            mt_scr[slot, c] = jnp.max(s, axis=0, keepdims=True)

    def fold(j, slot):
        off = pl.multiple_of(j * tile, tile)
        for c, (b, hh) in enumerate(chains):
            m = m_scr[c]
            m_new = jnp.maximum(m, mt_scr[slot, c])
            p = jnp.exp2(s_scr[slot, c] - m_new).astype(BF16)
            vt = vt_ref[b, hh * V_ROWS:(hh + 1) * V_ROWS, pl.ds(off, tile)]
            acc_scr[c] = jnp.exp2(m - m_new) * acc_scr[c] + _dot(vt, p)
            m_scr[c] = m_new

    m_scr[...] = jnp.full(m_scr.shape, NEG_INF, F32)
    acc_scr[...] = jnp.zeros(acc_scr.shape, F32)
    key = lax.broadcasted_iota(jnp.int32, (tile, tile), 0)
    qry = lax.broadcasted_iota(jnp.int32, (tile, tile), 1)
    logits(i, 0, key <= qry)
    n_tiles = i + 1
    visit = lambda k: jnp.where(k == 0, i, k - 1)

    def two_steps(u, _):
        k = 2 * u
        logits(visit(k + 1), 1, None)
        fold(visit(k), 0)
        logits(visit(k + 2), 0, None)
        fold(visit(k + 1), 1)
        return 0

    n_pairs = (n_tiles - 1) // 2
    lax.fori_loop(0, n_pairs, two_steps, 0)
    k = 2 * n_pairs

    @pl.when(n_tiles - k == 2)
    def _():
        logits(visit(k + 1), 1, None)
        fold(visit(k), 0)
        fold(visit(k + 1), 1)

    @pl.when(n_tiles - k == 1)
    def _():
        fold(visit(k), 0)

    for b in range(qt_ref.shape[0]):
        o_t = jnp.concatenate([acc_scr[c, :HEAD_DIM, :] / acc_scr[c, HEAD_DIM:HEAD_DIM + 1, :]
                               for c in (2 * b, 2 * b + 1)], axis=0)
        o_ref[b] = o_t.T


def _prompt_attention(qt, ka, vtb, n_seq, seq_len):
    tile = min(Q_TILE, seq_len)
    assert seq_len % tile == 0
    pair = 2 * HEAD_PAD
    ka3 = ka.reshape(n_seq, seq_len, N_HEADS * HEAD_PAD)
    nb = FLASH_SEQS if n_seq % FLASH_SEQS == 0 else 1
    n_chains = 2 * nb
    o = pl.pallas_call(
        functools.partial(_flash_kernel, tile=tile),
        grid=(n_seq // nb, N_HEADS // 2, seq_len // tile),
        in_specs=[pl.BlockSpec((nb, pair, tile), lambda b, hp, i: (b, hp, i)),
                  pl.BlockSpec((nb, seq_len, pair), lambda b, hp, i: (b, 0, hp)),
                  pl.BlockSpec((nb, 2 * V_ROWS, seq_len), lambda b, hp, i: (b, hp, 0))],
        out_specs=pl.BlockSpec((nb, tile, 2 * HEAD_DIM), lambda b, hp, i: (b, i, hp)),
        out_shape=jax.ShapeDtypeStruct((n_seq, seq_len, D_ATT), F32),
        scratch_shapes=[pltpu.VMEM((2, n_chains, tile, tile), F32),
                        pltpu.VMEM((2, n_chains, 1, tile), F32),
                        pltpu.VMEM((n_chains, 1, tile), F32),
                        pltpu.VMEM((n_chains, V_ROWS, tile), F32)],
        compiler_params=_params(3),
        name="prompt_attention",
    )(qt, ka3, vtb)
    return o.reshape(n_seq * seq_len, D_ATT)


def _online_update(s, pv_of, m_s, l_s, acc_s):
    m_prev = m_s[...]
    m_new = jnp.maximum(m_prev, jnp.max(s, axis=-1, keepdims=True))
    alpha = jnp.exp(m_prev - m_new)
    p = jnp.exp(s - m_new)
    l_s[...] = alpha * l_s[...] + jnp.sum(p, axis=-1, keepdims=True)
    acc_s[...] = alpha * acc_s[...] + pv_of(p.astype(BF16))
    m_s[...] = m_new


def _suffix_sum_lanes(x):
    n = x.shape[1]
    lane = lax.broadcasted_iota(jnp.int32, x.shape, 1)
    y = jnp.where(lane + 1 < n, pltpu.roll(x, n - 1, axis=1), 0.0)
    shift = 1
    while shift < n:
        y = y + jnp.where(lane + shift < n, pltpu.roll(y, n - shift, axis=1), 0.0)
        shift *= 2
    return y


def _per_head_rows(bias, seq_len):
    return jnp.concatenate(
        [jnp.broadcast_to(bias[h:h + 1, :], (seq_len, bias.shape[1])) for h in range(N_HEADS)], axis=0)


def _sample_attn_kernel(pt_ref, q_ref, cnq_ref, kn_ref, vn_ref, cnk_ref, *refs, ppc):
    del pt_ref
    k_refs, v_refs, f_refs = refs[:ppc], refs[ppc:2 * ppc], refs[2 * ppc:3 * ppc]
    o_ref, qbd_s, m_s, l_s, acc_s, tot_s = refs[3 * ppc:]
    seq_len = q_ref.shape[0]
    rows = N_HEADS * seq_len
    step = pl.program_id(1)
    row_head = _seq_index(lax.broadcasted_iota(jnp.int32, (rows, D_ATT), 0), seq_len)[0]
    own_head = row_head == _seq_index(lax.broadcasted_iota(jnp.int32, (rows, D_ATT), 1), HEAD_DIM)[0]

    @pl.when(step == 0)
    def _():
        q_rows = jnp.concatenate([q_ref[...]] * N_HEADS, axis=0)
        qbd_s[...] = jnp.where(own_head, q_rows, 0.0).astype(BF16)
        m_s[...] = jnp.full(m_s.shape, NEG_INF, F32)
        l_s[...] = jnp.zeros(l_s.shape, F32)
        acc_s[...] = jnp.zeros(acc_s.shape, F32)
        tot_s[...] = jnp.zeros(tot_s.shape, F32)

    page = k_refs[0].shape[3]
    kc = jnp.concatenate([r[0].reshape(D_ATT, page) for r in k_refs], axis=1).astype(BF16)
    vc = jnp.concatenate([r[0].reshape(D_ATT, page) for r in v_refs], axis=1).astype(BF16)

    after = tot_s[...]
    rest_pages = [None] * ppc
    for pg in reversed(range(ppc)):
        logf_pg = f_refs[pg][0]
        rest_pages[pg] = _suffix_sum_lanes(logf_pg) + after
        after = after + jnp.sum(logf_pg, axis=1, keepdims=True)
    tot_s[...] = after
    rest = jnp.concatenate(rest_pages, axis=1)

    s = _dot(qbd_s[...], kc) + _per_head_rows(rest, seq_len) + cnq_ref[0]
    _online_update(s, lambda p: _dot_nt(p, vc), m_s, l_s, acc_s)

    @pl.when(step == pl.num_programs(1) - 1)
    def _():
        no_keys = jnp.zeros((LANES - seq_len, D_ATT), F32)
        kn = jnp.concatenate([kn_ref[...], no_keys], axis=0).astype(BF16)
        vn = jnp.concatenate([vn_ref[...], no_keys], axis=0).astype(BF16)
        s2 = _dot_nt(qbd_s[...], kn) - _per_head_rows(cnk_ref[0], seq_len) + cnq_ref[0]
        t = _seq_index(lax.broadcasted_iota(jnp.int32, s2.shape, 0), seq_len)[1]
        sp = lax.broadcasted_iota(jnp.int32, s2.shape, 1)
        s2 = jnp.where(sp <= t, s2, NEG_INF)
        _online_update(s2, lambda p: _dot(p, vn), m_s, l_s, acc_s)
        o_blocks = jnp.where(own_head, acc_s[...] / l_s[...], 0.0)
        o_ref[...] = functools.reduce(
            lambda a, b: a + b, [o_blocks[h * seq_len:(h + 1) * seq_len, :] for h in range(N_HEADS)])


def _sample_attention(q, k_new, v_new, c_new, cache_k, cache_v, cache_logf, page_table, seq_len):
    n = q.shape[0]
    n_seq = n // seq_len
    n_pages = page_table.shape[1]
    page = cache_k.shape[1]
    ppc = PAGES_PER_STEP
    assert n_pages % ppc == 0 and seq_len <= LANES
    n_chunks = n_pages // ppc

    cache_kt = cache_k.transpose(0, 2, 3, 1)
    cache_vt = cache_v.transpose(0, 2, 3, 1)
    cache_ft = cache_logf.transpose(0, 2, 1)

    rows = N_HEADS * seq_len
    cn = c_new[:, :N_HEADS].reshape(n_seq, seq_len, N_HEADS).transpose(0, 2, 1)
    cnq = cn.reshape(n_seq, rows, 1)
    cnk = jnp.pad(cn, ((0, 0), (0, 0), (0, LANES - seq_len)))

    def page_map(j):
        return lambda b, c, pt: (pt[b, (n_chunks - 1 - c) * ppc + j], 0, 0, 0)

    def logf_map(j):
        return lambda b, c, pt: (pt[b, (n_chunks - 1 - c) * ppc + j], 0, 0)

    seq_rows = pl.BlockSpec((seq_len, D_ATT), lambda b, c, pt: (b, 0))
    kv_page = lambda j: pl.BlockSpec((1, N_HEADS, HEAD_DIM, page), page_map(j))
    return pl.pallas_call(
        functools.partial(_sample_attn_kernel, ppc=ppc),
        grid_spec=pltpu.PrefetchScalarGridSpec(
            num_scalar_prefetch=1,
            grid=(n_seq, n_chunks),
            in_specs=[seq_rows, pl.BlockSpec((1, rows, 1), lambda b, c, pt: (b, 0, 0)),
                      seq_rows, seq_rows,
                      pl.BlockSpec((1, N_HEADS, LANES), lambda b, c, pt: (b, 0, 0))]
            + [kv_page(j) for j in range(ppc)] + [kv_page(j) for j in range(ppc)]
            + [pl.BlockSpec((1, N_HEADS, page), logf_map(j)) for j in range(ppc)],
            out_specs=seq_rows,
            scratch_shapes=[pltpu.VMEM((rows, D_ATT), BF16),
                            pltpu.VMEM((rows, 1), F32), pltpu.VMEM((rows, 1), F32),
                            pltpu.VMEM((rows, D_ATT), F32),
                            pltpu.VMEM((N_HEADS, 1), F32)]),
        out_shape=jax.ShapeDtypeStruct((n, D_ATT), F32),
        compiler_params=_params(2),
        name="sample_attention",
    )(page_table, q, cnq, k_new, v_new,
      cnk, *([cache_kt] * ppc), *([cache_vt] * ppc), *([cache_ft] * ppc))


def _pad_heads(w):
    d = w.shape[0]
    w = w.reshape(d, N_HEADS, HEAD_DIM)
    return jnp.pad(w, ((0, 0), (0, 0), (0, HEAD_PAD - HEAD_DIM))).reshape(d, N_HEADS * HEAD_PAD)


def _k_bias_placement():
    rows = np.zeros((LANES, N_HEADS * HEAD_PAD), np.float32)
    for h in range(N_HEADS):
        for part in range(N_PARTS):
            rows[part * N_HEADS + h, h * HEAD_PAD + HEAD_DIM + part] = -1.0
            rows[N_PARTS * N_HEADS, h * HEAD_PAD + HEAD_DIM + N_PARTS + part] = 1.0
    return jnp.asarray(rows, BF16)


def _q_bias_placement():
    place = np.zeros((N_HEADS * BF16_ROWS, N_PARTS * N_HEADS + BF16_ROWS), np.float32)
    for h in range(N_HEADS):
        for part in range(N_PARTS):
            place[h * BF16_ROWS + part, N_PARTS * N_HEADS] = 1.0
            place[h * BF16_ROWS + N_PARTS + part, part * N_HEADS + h] = 1.0
    return jnp.asarray(place, BF16)


def _prepare_weights(norm_a, w_in_a, conv_w_a, w_out_a, norm_kv, w_kv, b_f, norm_b, w_in_b, w_out_b,
                     w_ple_proj, w_ple_gate, norm_f):
    n_a, n_b = w_in_a.shape[0], w_in_b.shape[0]
    scale = HEAD_DIM ** -0.5
    k_rows = _k_bias_placement()
    q_place = _q_bias_placement()
    ple = lambda i: dict(w_gate=w_ple_gate[i].astype(BF16), w_proj=w_ple_proj[i].astype(BF16))
    conv = [dict(g=norm_a[i][None, :], w_in=w_in_a[i].astype(BF16), conv_w=conv_w_a[i],
                 w_out=w_out_a[i].astype(BF16), **ple(i)) for i in range(n_a)]
    w_k, w_v, w_f = w_kv[:, :D_ATT], w_kv[:, D_ATT:2 * D_ATT], w_kv[:, 2 * D_ATT:]
    w_f_pad = jnp.pad(w_f, ((0, 0), (0, LANES - N_HEADS)))
    kv = dict(g_kv=norm_kv[None, :],
              w_kvf=jnp.concatenate([w_k, w_v, w_f_pad], axis=1).astype(BF16),
              b_f=jnp.pad(b_f, (0, LANES - N_HEADS))[None, :],
              b_f_col=b_f[:, None],
              w_k_aug=jnp.concatenate([_pad_heads(w_k).astype(BF16), k_rows], axis=0),
              w_kt=w_k.T.astype(BF16), w_vt=w_v.T.astype(BF16), w_ft=w_f.T.astype(BF16))
    attn = []
    for j in range(n_b):
        w_q = w_in_b[j][:, :D_ATT] * scale
        attn.append(dict(g=norm_b[j][None, :], w_q=w_q.astype(BF16),
                         w_qt=(w_q * LOG2_E).T.astype(BF16), q_bias_place=q_place,
                         w_z=w_in_b[j][:, D_ATT:].astype(BF16),
                         w_out=w_out_b[j].astype(BF16), **ple(n_a + j)))
    return conv, kv, attn, norm_f[None, :]


def _trunk(x3, p4, conv_state, weights, attend):
    conv_w, kv_w, attn_w, g_final = weights
    n_seq, seq_len, d = x3.shape
    n = n_seq * seq_len
    long_seq = conv_state is None
    x = x3.reshape(n, d)
    p = p4.reshape(p4.shape[0], n, p4.shape[-1])
    new_conv = []
    for i, w in enumerate(conv_w):
        x, st = _conv_layer(x, p, i, None if long_seq else conv_state[i], w, seq_len)
        new_conv.append(st)
    kv = _shared_kv(x, kv_w, n_seq, seq_len, long_seq)
    for j, w in enumerate(attn_w):
        q, sz = _attn_in(x, kv["c"], w, n_seq, seq_len, long_seq)
        o = attend(q, kv)
        x = _attn_out(x, o, sz, p, len(conv_w) + j, w, g_final, final=(j == len(attn_w) - 1))
    return x.reshape(n_seq, seq_len, d), jnp.stack(new_conv), kv["k"], kv["v"], kv["logf"]


def kernel(x_prompt, x_sample, p_prompt, p_sample, state_conv, cache_k, cache_v, cache_logf, page_table, norm_a, w_in_a, conv_w_a, w_out_a, norm_kv, w_kv, b_f, norm_b, w_in_b, w_out_b, w_ple_proj, w_ple_gate, norm_f):
    assert conv_w_a.shape[1] == CONV_W and w_kv.shape[1] == 2 * D_ATT + N_HEADS
    weights = _prepare_weights(norm_a, w_in_a, conv_w_a, w_out_a, norm_kv, w_kv, b_f, norm_b, w_in_b,
                               w_out_b, w_ple_proj, w_ple_gate, norm_f)
    n_prompt, prompt_len = x_prompt.shape[:2]
    sample_len = x_sample.shape[1]

    def prompt_attend(qt, kv):
        return _prompt_attention(qt, kv["ka"], kv["vtb"], n_prompt, prompt_len)

    def sample_attend(q, kv):
        return _sample_attention(q, kv["k2"], kv["v2"], kv["c"], cache_k, cache_v, cache_logf,
                                 page_table, sample_len)

    y_p, conv_p, k_p, v_p, logf_p = _trunk(x_prompt, p_prompt, None, weights, prompt_attend)
    y_s, conv_s, k_s, v_s, logf_s = _trunk(x_sample, p_sample, state_conv, weights, sample_attend)
    return (y_p, y_s, conv_p, k_p, v_p, logf_p, conv_s, k_s, v_s, logf_s)
```

```python
import functools

import numpy as np
import jax
import jax.numpy as jnp
from jax import lax
from jax.experimental import pallas as pl
from jax.experimental.pallas import tpu as pltpu

F32 = jnp.float32
BF16 = jnp.bfloat16

N_HEADS = 16
HEAD_DIM = 64
D_ATT = N_HEADS * HEAD_DIM
CONV_W = 3
EPS = 1e-6
NEG_INF = -1e30
LOG2_E = 1.4426950408889634

LANES = 128
SUBLANES = 8
HEAD_PAD = LANES
N_PARTS = 3
BF16_ROWS = 16
V_ROWS = HEAD_DIM + BF16_ROWS
VMEM_LIMIT_BYTES = 56 * 2**20

ROW_TILE = 1024
CONV_ROW_TILE = 256
Q_TILE = 512
FLASH_SEQS = 2
PAGES_PER_STEP = 8
DENSE_KEYS_PER_STEP = 4096
_HIGHEST = lax.Precision.HIGHEST


def _params(n_grid_dims):
    return pltpu.CompilerParams(
        dimension_semantics=("arbitrary",) * n_grid_dims,
        vmem_limit_bytes=VMEM_LIMIT_BYTES)


def _resident(shape):
    zeros = (0,) * len(shape)
    return pl.BlockSpec(shape, lambda *_: zeros, pipeline_mode=pl.Buffered(1))


def _rows(tm, width):
    return pl.BlockSpec((tm, width), lambda i: (i, 0))


def _dot(a, b):
    return jnp.dot(a, b, preferred_element_type=F32)


def _rmsnorm(x, g):
    return x * lax.rsqrt(jnp.mean(x * x, axis=-1, keepdims=True) + EPS) * g


def _silu(z):
    return z * jax.nn.sigmoid(z)


def _log_sigmoid(x):
    return jnp.minimum(x, 0.0) - jnp.log1p(jnp.exp(-jnp.abs(x)))


def _ple_residual(x1, p_ref, wg_ref, wp_ref):
    gate = jax.nn.sigmoid(_dot(x1.astype(BF16), wg_ref[...]))
    proj = _dot(p_ref[...].astype(BF16), wp_ref[...])
    return x1 + gate * proj


def _seq_index(idx, seq_len):
    if seq_len & (seq_len - 1) == 0:
        shift = seq_len.bit_length() - 1
        return lax.shift_right_logical(idx, shift), idx & (seq_len - 1)
    return idx // seq_len, idx % seq_len


def _split_bf16(c):
    hi = c.astype(BF16)
    r1 = c - hi.astype(F32)
    mid = r1.astype(BF16)
    lo = (r1 - mid.astype(F32)).astype(BF16)
    return hi, mid, lo


def _bias_pieces_t(ct):
    tm = ct.shape[1]
    ones_row = jnp.where(lax.broadcasted_iota(jnp.int32, (BF16_ROWS, tm), 0) == 0, 1.0, 0.0).astype(BF16)
    return jnp.concatenate(list(_split_bf16(ct)) + [ones_row], axis=0)


def _prefix_sum_lanes(x):
    n = x.shape[1]
    lane = lax.broadcasted_iota(jnp.int32, x.shape, 1)
    shift = 1
    while shift < n:
        x = x + jnp.where(lane >= shift, pltpu.roll(x, shift, axis=1), 0.0)
        shift *= 2
    return x


def _conv_core(x_ref, p_ref, g_ref, win_ref, cw_ref, wout_ref, wg_ref, wp_ref, xo_ref, ubuf,
               fix_first_rows):
    tm, d = x_ref.shape
    x = x_ref[...]
    h = _rmsnorm(x, g_ref[...]).astype(BF16)
    proj = _dot(h, win_ref[...])
    b_gate, c_gate, xin, z = (proj[:, k * d:(k + 1) * d] for k in range(4))
    u = c_gate * xin
    ubuf[SUBLANES:SUBLANES + tm, :] = u
    u1 = ubuf[SUBLANES - 1:SUBLANES - 1 + tm, :]
    u2 = ubuf[SUBLANES - 2:SUBLANES - 2 + tm, :]
    u1, u2 = fix_first_rows(u1, u2)
    cw = cw_ref[...]
    conv = cw[0:1, :] * u2 + cw[1:2, :] * u1 + cw[2:3, :] * u
    y = b_gate * conv * _silu(z)
    x1 = x + _dot(y.astype(BF16), wout_ref[...])
    xo_ref[...] = _ple_residual(x1, p_ref, wg_ref, wp_ref)
    return u


def _conv_long_kernel(x_ref, p_ref, g_ref, win_ref, cw_ref, wout_ref, wg_ref, wp_ref,
                      xo_ref, tail_ref, ubuf, *, tiles_per_seq):
    tm, d = x_ref.shape
    first = (pl.program_id(0) % tiles_per_seq) == 0

    @pl.when(first)
    def _():
        ubuf[0:SUBLANES, :] = jnp.zeros((SUBLANES, d), F32)

    @pl.when(jnp.logical_not(first))
    def _():
        ubuf[0:SUBLANES, :] = ubuf[tm:tm + SUBLANES, :]

    _conv_core(x_ref, p_ref, g_ref, win_ref, cw_ref, wout_ref, wg_ref, wp_ref, xo_ref, ubuf,
               lambda u1, u2: (u1, u2))
    tail_ref[0] = ubuf[tm:tm + SUBLANES, :]


def _conv_short_kernel(x_ref, p_ref, s1_ref, s2_ref, g_ref, win_ref, cw_ref, wout_ref, wg_ref,
                       wp_ref, xo_ref, u_ref, ubuf, *, seq_len):
    tm, d = x_ref.shape
    ubuf[0:SUBLANES, :] = jnp.zeros((SUBLANES, d), F32)
    _, t = _seq_index(lax.broadcasted_iota(jnp.int32, (tm, d), 0), seq_len)

    def fix(u1, u2):
        return (jnp.where(t < 1, s1_ref[...], u1), jnp.where(t < 2, s2_ref[...], u2))

    u_ref[...] = _conv_core(x_ref, p_ref, g_ref, win_ref, cw_ref, wout_ref, wg_ref, wp_ref,
                            xo_ref, ubuf, fix)


def _layer_rows(tm, p_all, layer):
    return pl.BlockSpec((None, tm, p_all.shape[2]), lambda i: (layer, i, 0))


def _conv_layer(x, p_all, layer, state, w, seq_len):
    n, d = x.shape
    n_seq = n // seq_len
    weights = (w["g"], w["w_in"], w["conv_w"], w["w_out"], w["w_gate"], w["w_proj"])
    w_specs = [_resident(a.shape) for a in weights]
    if state is None:
        tm = min(CONV_ROW_TILE, seq_len)
        assert seq_len % tm == 0 and tm % SUBLANES == 0
        x_new, tails = pl.pallas_call(
            functools.partial(_conv_long_kernel, tiles_per_seq=seq_len // tm),
            grid=(n // tm,),
            in_specs=[_rows(tm, d), _layer_rows(tm, p_all, layer)] + w_specs,
            out_specs=[_rows(tm, d), pl.BlockSpec((1, SUBLANES, d), lambda i: (i, 0, 0))],
            out_shape=[jax.ShapeDtypeStruct((n, d), F32),
                       jax.ShapeDtypeStruct((n // tm, SUBLANES, d), F32)],
            scratch_shapes=[pltpu.VMEM((tm + SUBLANES, d), F32)],
            compiler_params=_params(1),
            name="conv_layer_prompt",
        )(x, p_all, *weights)
        last = tails.reshape(n_seq, seq_len // tm, SUBLANES, d)[:, -1]
        return x_new, last[:, SUBLANES - (CONV_W - 1):, :]
    tm = n
    assert tm % seq_len == 0 and seq_len >= CONV_W - 1
    zeros = jnp.zeros((n_seq, seq_len, d), F32)
    s1 = zeros.at[:, 0].set(state[:, 1]).reshape(n, d)
    s2 = zeros.at[:, 0].set(state[:, 0]).at[:, 1].set(state[:, 1]).reshape(n, d)
    x_new, u = pl.pallas_call(
        functools.partial(_conv_short_kernel, seq_len=seq_len),
        grid=(n // tm,),
        in_specs=[_rows(tm, d), _layer_rows(tm, p_all, layer), _rows(tm, d), _rows(tm, d)] + w_specs,
        out_specs=[_rows(tm, d), _rows(tm, d)],
        out_shape=[jax.ShapeDtypeStruct((n, d), F32), jax.ShapeDtypeStruct((n, d), F32)],
        scratch_shapes=[pltpu.VMEM((tm + SUBLANES, d), F32)],
        compiler_params=_params(1),
        name="conv_layer_sample",
    )(x, p_all, s1, s2, *weights)
    return x_new, u.reshape(n_seq, seq_len, d)[:, seq_len - (CONV_W - 1):, :]


def _cumsum_rows(vals, seq_len):
    tm = vals.shape[0]
    r = lax.broadcasted_iota(jnp.int32, (tm, tm), 0)
    c = lax.broadcasted_iota(jnp.int32, (tm, tm), 1)
    if seq_len < tm:
        lower = jnp.where(c <= r, _seq_index(c, seq_len)[0], -1)
        tri = jnp.where(lower == _seq_index(r, seq_len)[0], 1.0, 0.0)
    else:
        tri = jnp.where(c <= r, 1.0, 0.0)
    return jnp.dot(tri.astype(F32), vals, preferred_element_type=F32, precision=_HIGHEST)


def _dot_nt(a, b):
    return lax.dot_general(a, b, (((1,), (1,)), ((), ())), preferred_element_type=F32)


def _kv_long_kernel(x_ref, g_ref, wka_ref, wkt_ref, wvt_ref, wft_ref, bfc_ref,
                    kt_ref, vt_ref, logft_ref, ct_ref, ka_ref, vtb_ref, carry, *, tiles_per_seq):
    tm = x_ref.shape[0]

    @pl.when((pl.program_id(0) % tiles_per_seq) == 0)
    def _():
        carry[...] = jnp.zeros(carry.shape, F32)

    h = _rmsnorm(x_ref[...], g_ref[...]).astype(BF16)
    logf_t = _log_sigmoid(_dot_nt(wft_ref[...], h) + bfc_ref[...])
    logft_ref[0] = logf_t
    run = carry[:, 0:1]
    blocks = []
    for b in range(tm // LANES):
        blocks.append(_prefix_sum_lanes(logf_t[:, b * LANES:(b + 1) * LANES]) + run)
        run = blocks[-1][:, LANES - 1:LANES]
    carry[...] = jnp.broadcast_to(run, carry.shape)
    c2_t = jnp.concatenate(blocks, axis=1) * LOG2_E
    ct_ref[0] = c2_t
    pieces_t = _bias_pieces_t(c2_t).astype(F32)
    pieces = jnp.concatenate([pieces_t, jnp.zeros((LANES - pieces_t.shape[0], tm), F32)], axis=0).T
    lhs = jnp.concatenate([h, pieces.astype(BF16)], axis=1)
    ka_ref[...] = _dot(lhs, wka_ref[...]).astype(BF16)
    kt_ref[0] = _dot_nt(wkt_ref[...], h)
    vt = _dot_nt(wvt_ref[...], h)
    vt_ref[0] = vt
    ones_row = jnp.where(lax.broadcasted_iota(jnp.int32, (BF16_ROWS, tm), 0) == 0, 1.0, 0.0).astype(BF16)
    for hd in range(N_HEADS):
        vtb_ref[0, hd * V_ROWS:hd * V_ROWS + HEAD_DIM, :] = (
            vt[hd * HEAD_DIM:(hd + 1) * HEAD_DIM, :].astype(BF16))
        vtb_ref[0, hd * V_ROWS + HEAD_DIM:(hd + 1) * V_ROWS, :] = ones_row


def _kv_short_kernel(x_ref, g_ref, wa_ref, bf_ref, k_ref, v_ref, logf_ref, c_ref, *, seq_len):
    h = _rmsnorm(x_ref[...], g_ref[...]).astype(BF16)
    r = _dot(h, wa_ref[...])
    k_ref[...] = r[:, :D_ATT]
    v_ref[...] = r[:, D_ATT:2 * D_ATT]
    logf = _log_sigmoid(r[:, 2 * D_ATT:] + bf_ref[...])
    logf_ref[...] = logf[:, :N_HEADS]
    c_ref[...] = _cumsum_rows(logf, seq_len)


def _shared_kv(x, w, n_seq, seq_len, long_seq):
    n, d = x.shape
    if long_seq:
        tm = min(ROW_TILE, seq_len)
        assert seq_len % tm == 0
        tps = seq_len // tm
        weights = (w["g_kv"], w["w_k_aug"], w["w_kt"], w["w_vt"], w["w_ft"], w["b_f_col"])
        by_seq = lambda rows: pl.BlockSpec((1, rows, tm), lambda i: (i // tps, 0, i % tps))
        kt, vt, logft, ct, ka, vtb = pl.pallas_call(
            functools.partial(_kv_long_kernel, tiles_per_seq=tps),
            grid=(n // tm,),
            in_specs=[_rows(tm, d)] + [_resident(a.shape) for a in weights],
            out_specs=[by_seq(D_ATT), by_seq(D_ATT), by_seq(N_HEADS), by_seq(N_HEADS),
                       _rows(tm, N_HEADS * HEAD_PAD), by_seq(N_HEADS * V_ROWS)],
            out_shape=[jax.ShapeDtypeStruct((n_seq, D_ATT, seq_len), F32),
                       jax.ShapeDtypeStruct((n_seq, D_ATT, seq_len), F32),
                       jax.ShapeDtypeStruct((n_seq, N_HEADS, seq_len), F32),
                       jax.ShapeDtypeStruct((n_seq, N_HEADS, seq_len), F32),
                       jax.ShapeDtypeStruct((n, N_HEADS * HEAD_PAD), BF16),
                       jax.ShapeDtypeStruct((n_seq, N_HEADS * V_ROWS, seq_len), BF16)],
            scratch_shapes=[pltpu.VMEM((N_HEADS, LANES), F32)],
            compiler_params=_params(1),
            name="shared_kv_prompt",
        )(x, *weights)
        to_std = lambda a: a.reshape(n_seq, N_HEADS, HEAD_DIM, seq_len).transpose(0, 3, 1, 2)
        return dict(k=to_std(kt), v=to_std(vt), logf=logft.transpose(0, 2, 1), c=ct, ka=ka, vtb=vtb)
    tm = n
    assert tm % seq_len == 0
    weights = (w["g_kv"], w["w_kvf"], w["b_f"])
    k, v, logf, c = pl.pallas_call(
        functools.partial(_kv_short_kernel, seq_len=seq_len),
        grid=(n // tm,),
        in_specs=[_rows(tm, d)] + [_resident(a.shape) for a in weights],
        out_specs=[_rows(tm, D_ATT), _rows(tm, D_ATT), _rows(tm, N_HEADS), _rows(tm, LANES)],
        out_shape=[jax.ShapeDtypeStruct((n, D_ATT), F32), jax.ShapeDtypeStruct((n, D_ATT), F32),
                   jax.ShapeDtypeStruct((n, N_HEADS), F32), jax.ShapeDtypeStruct((n, LANES), F32)],
        compiler_params=_params(1),
        name="shared_kv_sample",
    )(x, *weights)
    std = lambda a: a.reshape(n_seq, seq_len, N_HEADS, HEAD_DIM)
    return dict(k=std(k), v=std(v), logf=logf.reshape(n_seq, seq_len, N_HEADS), c=c, k2=k, v2=v)


def _attn_in_prompt_kernel(x_ref, ct_ref, g_ref, wqt_ref, wz_ref, place_ref, qt_ref, sz_ref):
    tm = x_ref.shape[0]
    h = _rmsnorm(x_ref[...], g_ref[...]).astype(BF16)
    q_t = _dot_nt(wqt_ref[...], h)
    bias_rows = _dot(place_ref[...], _bias_pieces_t(ct_ref[0])).astype(BF16)
    no_rows = jnp.zeros((HEAD_PAD - HEAD_DIM - BF16_ROWS, tm), BF16)
    for hd in range(N_HEADS):
        r0 = hd * HEAD_PAD
        qt_ref[0, r0:r0 + HEAD_DIM, :] = q_t[hd * HEAD_DIM:(hd + 1) * HEAD_DIM, :].astype(BF16)
        qt_ref[0, r0 + HEAD_DIM:r0 + HEAD_DIM + BF16_ROWS, :] = bias_rows[hd * BF16_ROWS:(hd + 1) * BF16_ROWS, :]
        qt_ref[0, r0 + HEAD_DIM + BF16_ROWS:r0 + HEAD_PAD, :] = no_rows
    sz_ref[...] = _silu(_dot(h, wz_ref[...]))


def _attn_in_sample_kernel(x_ref, g_ref, wq_ref, wz_ref, q_ref, sz_ref):
    h = _rmsnorm(x_ref[...], g_ref[...]).astype(BF16)
    q_ref[...] = _dot(h, wq_ref[...])
    sz_ref[...] = _silu(_dot(h, wz_ref[...]))


def _attn_in(x, c, w, n_seq, seq_len, long_seq):
    n, d = x.shape
    if long_seq:
        tm = min(ROW_TILE, seq_len)
        assert seq_len % tm == 0
        tps = seq_len // tm
        weights = (w["g"], w["w_qt"], w["w_z"], w["q_bias_place"])
        return pl.pallas_call(
            _attn_in_prompt_kernel,
            grid=(n // tm,),
            in_specs=[_rows(tm, d), pl.BlockSpec((1, N_HEADS, tm), lambda i: (i // tps, 0, i % tps))]
            + [_resident(a.shape) for a in weights],
            out_specs=[pl.BlockSpec((1, N_HEADS * HEAD_PAD, tm), lambda i: (i // tps, 0, i % tps)),
                       _rows(tm, D_ATT)],
            out_shape=[jax.ShapeDtypeStruct((n_seq, N_HEADS * HEAD_PAD, seq_len), BF16),
                       jax.ShapeDtypeStruct((n, D_ATT), F32)],
            compiler_params=_params(1),
            name="attn_in_prompt",
        )(x, c, *weights)
    tm = min(ROW_TILE, n)
    weights = (w["g"], w["w_q"], w["w_z"])
    return pl.pallas_call(
        _attn_in_sample_kernel,
        grid=(n // tm,),
        in_specs=[_rows(tm, d)] + [_resident(a.shape) for a in weights],
        out_specs=[_rows(tm, D_ATT), _rows(tm, D_ATT)],
        out_shape=[jax.ShapeDtypeStruct((n, D_ATT), F32), jax.ShapeDtypeStruct((n, D_ATT), F32)],
        compiler_params=_params(1),
        name="attn_in_sample",
    )(x, *weights)


def _attn_out_kernel(x_ref, o_ref, sz_ref, p_ref, wout_ref, wg_ref, wp_ref, gf_ref, out_ref, *,
                     final):
    y = (o_ref[...] * sz_ref[...]).astype(BF16)
    x1 = x_ref[...] + _dot(y, wout_ref[...])
    x_new = _ple_residual(x1, p_ref, wg_ref, wp_ref)
    out_ref[...] = _rmsnorm(x_new, gf_ref[...]) if final else x_new


def _attn_out(x, o, sz, p_all, layer, w, g_final, final):
    n, d = x.shape
    tm = min(ROW_TILE, n)
    weights = (w["w_out"], w["w_gate"], w["w_proj"], g_final)
    return pl.pallas_call(
        functools.partial(_attn_out_kernel, final=final),
        grid=(n // tm,),
        in_specs=[_rows(tm, d), _rows(tm, D_ATT), _rows(tm, D_ATT), _layer_rows(tm, p_all, layer)]
        + [_resident(a.shape) for a in weights],
        out_specs=_rows(tm, d),
        out_shape=jax.ShapeDtypeStruct((n, d), F32),
        compiler_params=_params(1),
        name="attn_out",
    )(x, o, sz, p_all, *weights)


def _flash_kernel(qt_ref, ka_ref, vt_ref, o_ref, s_scr, mt_scr, m_scr, acc_scr, *, tile):
    i = pl.program_id(2)
    chains = [(b, hh) for b in range(qt_ref.shape[0]) for hh in range(2)]
    qts = [qt_ref[b, hh * HEAD_PAD:(hh + 1) * HEAD_PAD, :] for b, hh in chains]

    def logits(j, slot, mask):
        off = pl.multiple_of(j * tile, tile)
        for c, (b, hh) in enumerate(chains):
            s = _dot(ka_ref[b, pl.ds(off, tile), hh * HEAD_PAD:(hh + 1) * HEAD_PAD], qts[c])
            if mask is not None:
                s = jnp.where(mask, s, NEG_INF)
            s_scr[slot, c] = s
            mt_scr[slot, c] = jnp.max(s, axis=0, keepdims=True)

    def fold(j, slot):
        off = pl.multiple_of(j * tile, tile)
        for c, (b, hh) in enumerate(chains):
            m = m_scr[c]
            m_new = jnp.maximum(m, mt_scr[slot, c])
            p = jnp.exp2(s_scr[slot, c] - m_new).astype(BF16)
            vt = vt_ref[b, hh * V_ROWS:(hh + 1) * V_ROWS, pl.ds(off, tile)]
            acc_scr[c] = jnp.exp2(m - m_new) * acc_scr[c] + _dot(vt, p)
            m_scr[c] = m_new

    m_scr[...] = jnp.full(m_scr.shape, NEG_INF, F32)
    acc_scr[...] = jnp.zeros(acc_scr.shape, F32)
    key = lax.broadcasted_iota(jnp.int32, (tile, tile), 0)
    qry = lax.broadcasted_iota(jnp.int32, (tile, tile), 1)
    logits(i, 0, key <= qry)
    n_tiles = i + 1
    visit = lambda k: jnp.where(k == 0, i, k - 1)

    def two_steps(u, _):
        k = 2 * u
        logits(visit(k + 1), 1, None)
        fold(visit(k), 0)
        logits(visit(k + 2), 0, None)
        fold(visit(k + 1), 1)
        return 0

    n_pairs = (n_tiles - 1) // 2
    lax.fori_loop(0, n_pairs, two_steps, 0)
    k = 2 * n_pairs

    @pl.when(n_tiles - k == 2)
    def _():
        logits(visit(k + 1), 1, None)
        fold(visit(k), 0)
        fold(visit(k + 1), 1)

    @pl.when(n_tiles - k == 1)
    def _():
        fold(visit(k), 0)

    for b in range(qt_ref.shape[0]):
        o_t = jnp.concatenate([acc_scr[c, :HEAD_DIM, :] / acc_scr[c, HEAD_DIM:HEAD_DIM + 1, :]
                               for c in (2 * b, 2 * b + 1)], axis=0)
        o_ref[b] = o_t.T


def _prompt_attention(qt, ka, vtb, n_seq, seq_len):
    tile = min(Q_TILE, seq_len)
    assert seq_len % tile == 0
    pair = 2 * HEAD_PAD
    ka3 = ka.reshape(n_seq, seq_len, N_HEADS * HEAD_PAD)
    nb = FLASH_SEQS if n_seq % FLASH_SEQS == 0 else 1
    n_chains = 2 * nb
    o = pl.pallas_call(
        functools.partial(_flash_kernel, tile=tile),
        grid=(n_seq // nb, N_HEADS // 2, seq_len // tile),
        in_specs=[pl.BlockSpec((nb, pair, tile), lambda b, hp, i: (b, hp, i)),
                  pl.BlockSpec((nb, seq_len, pair), lambda b, hp, i: (b, 0, hp)),
                  pl.BlockSpec((nb, 2 * V_ROWS, seq_len), lambda b, hp, i: (b, hp, 0))],
        out_specs=pl.BlockSpec((nb, tile, 2 * HEAD_DIM), lambda b, hp, i: (b, i, hp)),
        out_shape=jax.ShapeDtypeStruct((n_seq, seq_len, D_ATT), F32),
        scratch_shapes=[pltpu.VMEM((2, n_chains, tile, tile), F32),
                        pltpu.VMEM((2, n_chains, 1, tile), F32),
                        pltpu.VMEM((n_chains, 1, tile), F32),
                        pltpu.VMEM((n_chains, V_ROWS, tile), F32)],
        compiler_params=_params(3),
        name="prompt_attention",
    )(qt, ka3, vtb)
    return o.reshape(n_seq * seq_len, D_ATT)


def _online_update(s, pv_of, m_s, l_s, acc_s):
    m_prev = m_s[...]
    m_new = jnp.maximum(m_prev, jnp.max(s, axis=-1, keepdims=True))
    alpha = jnp.exp(m_prev - m_new)
    p = jnp.exp(s - m_new)
    l_s[...] = alpha * l_s[...] + jnp.sum(p, axis=-1, keepdims=True)
    acc_s[...] = alpha * acc_s[...] + pv_of(p.astype(BF16))
    m_s[...] = m_new


def _suffix_sum_lanes(x):
    n = x.shape[1]
    lane = lax.broadcasted_iota(jnp.int32, x.shape, 1)
    y = jnp.where(lane + 1 < n, pltpu.roll(x, n - 1, axis=1), 0.0)
    shift = 1
    while shift < n:
        y = y + jnp.where(lane + shift < n, pltpu.roll(y, n - shift, axis=1), 0.0)
        shift *= 2
    return y


def _per_head_rows(bias, seq_len):
    return jnp.concatenate(
        [jnp.broadcast_to(bias[h:h + 1, :], (seq_len, bias.shape[1])) for h in range(N_HEADS)], axis=0)


def _sample_step(step, kc, vc, rest, q_ref, cnq_ref, kn_ref, vn_ref, cnk_ref, o_ref,
                 qbd_s, m_s, l_s, acc_s):
    seq_len = q_ref.shape[0]
    rows = N_HEADS * seq_len
    row_head = _seq_index(lax.broadcasted_iota(jnp.int32, (rows, D_ATT), 0), seq_len)[0]
    own_head = row_head == _seq_index(lax.broadcasted_iota(jnp.int32, (rows, D_ATT), 1), HEAD_DIM)[0]

    @pl.when(step == 0)
    def _():
        q_rows = jnp.concatenate([q_ref[...]] * N_HEADS, axis=0)
        qbd_s[...] = jnp.where(own_head, q_rows, 0.0).astype(BF16)
        m_s[...] = jnp.full(m_s.shape, NEG_INF, F32)
        l_s[...] = jnp.zeros(l_s.shape, F32)
        acc_s[...] = jnp.zeros(acc_s.shape, F32)

    s = _dot(qbd_s[...], kc) + _per_head_rows(rest, seq_len) + cnq_ref[0]
    _online_update(s, lambda p: _dot_nt(p, vc), m_s, l_s, acc_s)

    @pl.when(step == pl.num_programs(1) - 1)
    def _():
        no_keys = jnp.zeros((LANES - seq_len, D_ATT), F32)
        kn = jnp.concatenate([kn_ref[...], no_keys], axis=0).astype(BF16)
        vn = jnp.concatenate([vn_ref[...], no_keys], axis=0).astype(BF16)
        s2 = _dot_nt(qbd_s[...], kn) - _per_head_rows(cnk_ref[0], seq_len) + cnq_ref[0]
        t = _seq_index(lax.broadcasted_iota(jnp.int32, s2.shape, 0), seq_len)[1]
        sp = lax.broadcasted_iota(jnp.int32, s2.shape, 1)
        s2 = jnp.where(sp <= t, s2, NEG_INF)
        _online_update(s2, lambda p: _dot(p, vn), m_s, l_s, acc_s)
        o_blocks = jnp.where(own_head, acc_s[...] / l_s[...], 0.0)
        o_ref[...] = functools.reduce(
            lambda a, b: a + b, [o_blocks[h * seq_len:(h + 1) * seq_len, :] for h in range(N_HEADS)])


def _sample_attn_paged_kernel(pt_ref, q_ref, cnq_ref, kn_ref, vn_ref, cnk_ref, *refs, ppc, keep):
    del pt_ref
    k_refs, v_refs, f_refs = refs[:ppc], refs[ppc:2 * ppc], refs[2 * ppc:3 * ppc]
    n_out = 4 if keep else 1
    outs, (qbd_s, m_s, l_s, acc_s, tot_s) = refs[3 * ppc:3 * ppc + n_out], refs[3 * ppc + n_out:]
    step = pl.program_id(1)

    @pl.when(step == 0)
    def _():
        tot_s[...] = jnp.zeros(tot_s.shape, F32)

    page = k_refs[0].shape[3]
    if keep:
        for pg in range(ppc):
            outs[1][0, :, pg * page:(pg + 1) * page] = k_refs[pg][0].reshape(D_ATT, page).astype(BF16)
            outs[2][0, :, pg * page:(pg + 1) * page] = v_refs[pg][0].reshape(D_ATT, page).astype(BF16)
        kc, vc = outs[1][0], outs[2][0]
    else:
        kc = jnp.concatenate([r[0].reshape(D_ATT, page) for r in k_refs], axis=1).astype(BF16)
        vc = jnp.concatenate([r[0].reshape(D_ATT, page) for r in v_refs], axis=1).astype(BF16)

    after = tot_s[...]
    rest_pages = [None] * ppc
    for pg in reversed(range(ppc)):
        logf_pg = f_refs[pg][0]
        rest_pages[pg] = _suffix_sum_lanes(logf_pg) + after
        after = after + jnp.sum(logf_pg, axis=1, keepdims=True)
    tot_s[...] = after
    rest = jnp.concatenate(rest_pages, axis=1)

    if keep:
        outs[3][0] = rest
    _sample_step(step, kc, vc, rest, q_ref, cnq_ref, kn_ref, vn_ref, cnk_ref, outs[0],
                 qbd_s, m_s, l_s, acc_s)


def _sample_attn_dense_kernel(q_ref, cnq_ref, kn_ref, vn_ref, cnk_ref, kc_ref, vc_ref, rest_ref,
                              o_ref, qbd_s, m_s, l_s, acc_s):
    _sample_step(pl.program_id(1), kc_ref[0], vc_ref[0], rest_ref[0], q_ref, cnq_ref, kn_ref, vn_ref,
                 cnk_ref, o_ref, qbd_s, m_s, l_s, acc_s)


def _sample_scratch(rows):
    return [pltpu.VMEM((rows, D_ATT), BF16),
            pltpu.VMEM((rows, 1), F32), pltpu.VMEM((rows, 1), F32),
            pltpu.VMEM((rows, D_ATT), F32)]


def _new_token_bias(c_new, n_seq, seq_len):
    cn = c_new[:, :N_HEADS].reshape(n_seq, seq_len, N_HEADS).transpose(0, 2, 1)
    return cn.reshape(n_seq, N_HEADS * seq_len, 1), jnp.pad(cn, ((0, 0), (0, 0), (0, LANES - seq_len)))


def _sample_attention_dense(q, k_new, v_new, c_new, kept, seq_len):
    kc, vc, rest = kept
    n = q.shape[0]
    n_seq, _, past = kc.shape
    chunk = min(DENSE_KEYS_PER_STEP, past)
    assert past % chunk == 0
    rows = N_HEADS * seq_len
    cnq, cnk = _new_token_bias(c_new, n_seq, seq_len)
    seq_rows = pl.BlockSpec((seq_len, D_ATT), lambda b, c: (b, 0))
    keys = lambda r: pl.BlockSpec((1, r, chunk), lambda b, c: (b, 0, c))
    return pl.pallas_call(
        _sample_attn_dense_kernel,
        grid=(n_seq, past // chunk),
        in_specs=[seq_rows, pl.BlockSpec((1, rows, 1), lambda b, c: (b, 0, 0)), seq_rows, seq_rows,
                  pl.BlockSpec((1, N_HEADS, LANES), lambda b, c: (b, 0, 0)),
                  keys(D_ATT), keys(D_ATT), keys(N_HEADS)],
        out_specs=seq_rows,
        out_shape=jax.ShapeDtypeStruct((n, D_ATT), F32),
        scratch_shapes=_sample_scratch(rows),
        compiler_params=_params(2),
        name="sample_attention_dense",
    )(q, cnq, k_new, v_new, cnk, kc, vc, rest)


def _sample_attention(q, k_new, v_new, c_new, cache_k, cache_v, cache_logf, page_table, seq_len, keep):
    n = q.shape[0]
    n_seq = n // seq_len
    n_pages = page_table.shape[1]
    page = cache_k.shape[1]
    ppc = PAGES_PER_STEP
    assert n_pages % ppc == 0 and seq_len <= LANES
    n_chunks = n_pages // ppc

    cache_kt = cache_k.transpose(0, 2, 3, 1)
    cache_vt = cache_v.transpose(0, 2, 3, 1)
    cache_ft = cache_logf.transpose(0, 2, 1)

    rows = N_HEADS * seq_len
    chunk = ppc * page
    cnq, cnk = _new_token_bias(c_new, n_seq, seq_len)

    def page_map(j):
        return lambda b, c, pt: (pt[b, (n_chunks - 1 - c) * ppc + j], 0, 0, 0)

    def logf_map(j):
        return lambda b, c, pt: (pt[b, (n_chunks - 1 - c) * ppc + j], 0, 0)

    seq_rows = pl.BlockSpec((seq_len, D_ATT), lambda b, c, pt: (b, 0))
    kv_page = lambda j: pl.BlockSpec((1, N_HEADS, HEAD_DIM, page), page_map(j))
    kept = lambda r: pl.BlockSpec((1, r, chunk), lambda b, c, pt: (b, 0, n_chunks - 1 - c))
    past = n_pages * page
    out_specs, out_shape = [seq_rows], [jax.ShapeDtypeStruct((n, D_ATT), F32)]
    if keep:
        out_specs += [kept(D_ATT), kept(D_ATT), kept(N_HEADS)]
        out_shape += [jax.ShapeDtypeStruct((n_seq, D_ATT, past), BF16),
                      jax.ShapeDtypeStruct((n_seq, D_ATT, past), BF16),
                      jax.ShapeDtypeStruct((n_seq, N_HEADS, past), F32)]
    outs = pl.pallas_call(
        functools.partial(_sample_attn_paged_kernel, ppc=ppc, keep=keep),
        grid_spec=pltpu.PrefetchScalarGridSpec(
            num_scalar_prefetch=1,
            grid=(n_seq, n_chunks),
            in_specs=[seq_rows, pl.BlockSpec((1, rows, 1), lambda b, c, pt: (b, 0, 0)),
                      seq_rows, seq_rows,
                      pl.BlockSpec((1, N_HEADS, LANES), lambda b, c, pt: (b, 0, 0))]
            + [kv_page(j) for j in range(ppc)] + [kv_page(j) for j in range(ppc)]
            + [pl.BlockSpec((1, N_HEADS, page), logf_map(j)) for j in range(ppc)],
            out_specs=out_specs,
            scratch_shapes=_sample_scratch(rows) + [pltpu.VMEM((N_HEADS, 1), F32)]),
        out_shape=out_shape,
        compiler_params=_params(2),
        name="sample_attention",
    )(page_table, q, cnq, k_new, v_new,
      cnk, *([cache_kt] * ppc), *([cache_vt] * ppc), *([cache_ft] * ppc))
    return outs[0], (tuple(outs[1:]) if keep else None)


def _pad_heads(w):
    d = w.shape[0]
    w = w.reshape(d, N_HEADS, HEAD_DIM)
    return jnp.pad(w, ((0, 0), (0, 0), (0, HEAD_PAD - HEAD_DIM))).reshape(d, N_HEADS * HEAD_PAD)


def _k_bias_placement():
    rows = np.zeros((LANES, N_HEADS * HEAD_PAD), np.float32)
    for h in range(N_HEADS):
        for part in range(N_PARTS):
            rows[part * N_HEADS + h, h * HEAD_PAD + HEAD_DIM + part] = -1.0
            rows[N_PARTS * N_HEADS, h * HEAD_PAD + HEAD_DIM + N_PARTS + part] = 1.0
    return jnp.asarray(rows, BF16)


def _q_bias_placement():
    place = np.zeros((N_HEADS * BF16_ROWS, N_PARTS * N_HEADS + BF16_ROWS), np.float32)
    for h in range(N_HEADS):
        for part in range(N_PARTS):
            place[h * BF16_ROWS + part, N_PARTS * N_HEADS] = 1.0
            place[h * BF16_ROWS + N_PARTS + part, part * N_HEADS + h] = 1.0
    return jnp.asarray(place, BF16)


def _prepare_weights(norm_a, w_in_a, conv_w_a, w_out_a, norm_kv, w_kv, b_f, norm_b, w_in_b, w_out_b,
                     w_ple_proj, w_ple_gate, norm_f):
    n_a, n_b = w_in_a.shape[0], w_in_b.shape[0]
    scale = HEAD_DIM ** -0.5
    k_rows = _k_bias_placement()
    q_place = _q_bias_placement()
    ple = lambda i: dict(w_gate=w_ple_gate[i].astype(BF16), w_proj=w_ple_proj[i].astype(BF16))
    conv = [dict(g=norm_a[i][None, :], w_in=w_in_a[i].astype(BF16), conv_w=conv_w_a[i],
                 w_out=w_out_a[i].astype(BF16), **ple(i)) for i in range(n_a)]
    w_k, w_v, w_f = w_kv[:, :D_ATT], w_kv[:, D_ATT:2 * D_ATT], w_kv[:, 2 * D_ATT:]
    w_f_pad = jnp.pad(w_f, ((0, 0), (0, LANES - N_HEADS)))
    kv = dict(g_kv=norm_kv[None, :],
              w_kvf=jnp.concatenate([w_k, w_v, w_f_pad], axis=1).astype(BF16),
              b_f=jnp.pad(b_f, (0, LANES - N_HEADS))[None, :],
              b_f_col=b_f[:, None],
              w_k_aug=jnp.concatenate([_pad_heads(w_k).astype(BF16), k_rows], axis=0),
              w_kt=w_k.T.astype(BF16), w_vt=w_v.T.astype(BF16), w_ft=w_f.T.astype(BF16))
    attn = []
    for j in range(n_b):
        w_q = w_in_b[j][:, :D_ATT] * scale
        attn.append(dict(g=norm_b[j][None, :], w_q=w_q.astype(BF16),
                         w_qt=(w_q * LOG2_E).T.astype(BF16), q_bias_place=q_place,
                         w_z=w_in_b[j][:, D_ATT:].astype(BF16),
                         w_out=w_out_b[j].astype(BF16), **ple(n_a + j)))
    return conv, kv, attn, norm_f[None, :]


def _trunk(x3, p4, conv_state, weights, attend):
    conv_w, kv_w, attn_w, g_final = weights
    n_seq, seq_len, d = x3.shape
    n = n_seq * seq_len
    long_seq = conv_state is None
    x = x3.reshape(n, d)
    p = p4.reshape(p4.shape[0], n, p4.shape[-1])
    new_conv = []
    for i, w in enumerate(conv_w):
        x, st = _conv_layer(x, p, i, None if long_seq else conv_state[i], w, seq_len)
        new_conv.append(st)
    kv = _shared_kv(x, kv_w, n_seq, seq_len, long_seq)
    for j, w in enumerate(attn_w):
        q, sz = _attn_in(x, kv["c"], w, n_seq, seq_len, long_seq)
        o = attend(q, kv)
        x = _attn_out(x, o, sz, p, len(conv_w) + j, w, g_final, final=(j == len(attn_w) - 1))
    return x.reshape(n_seq, seq_len, d), jnp.stack(new_conv), kv["k"], kv["v"], kv["logf"]


def kernel(x_prompt, x_sample, p_prompt, p_sample, state_conv, cache_k, cache_v, cache_logf, page_table, norm_a, w_in_a, conv_w_a, w_out_a, norm_kv, w_kv, b_f, norm_b, w_in_b, w_out_b, w_ple_proj, w_ple_gate, norm_f):
    assert conv_w_a.shape[1] == CONV_W and w_kv.shape[1] == 2 * D_ATT + N_HEADS
    weights = _prepare_weights(norm_a, w_in_a, conv_w_a, w_out_a, norm_kv, w_kv, b_f, norm_b, w_in_b,
                               w_out_b, w_ple_proj, w_ple_gate, norm_f)
    n_prompt, prompt_len = x_prompt.shape[:2]
    sample_len = x_sample.shape[1]

    def prompt_attend(qt, kv):
        return _prompt_attention(qt, kv["ka"], kv["vtb"], n_prompt, prompt_len)

    n_attn_layers = w_in_b.shape[0]

    def sample_attend(q, kv):
        if "kept" in kv:
            return _sample_attention_dense(q, kv["k2"], kv["v2"], kv["c"], kv["kept"], sample_len)
        o, kept = _sample_attention(q, kv["k2"], kv["v2"], kv["c"], cache_k, cache_v, cache_logf,
                                    page_table, sample_len, keep=n_attn_layers > 1)
        if kept is not None:
            kv["kept"] = kept
        return o

    y_p, conv_p, k_p, v_p, logf_p = _trunk(x_prompt, p_prompt, None, weights, prompt_attend)
    y_s, conv_s, k_s, v_s, logf_s = _trunk(x_sample, p_sample, state_conv, weights, sample_attend)
    return (y_p, y_s, conv_p, k_p, v_p, logf_p, conv_s, k_s, v_s, logf_s)
```

```python
import functools

import numpy as np
import jax
import jax.numpy as jnp
from jax import lax
from jax.experimental import pallas as pl
from jax.experimental.pallas import tpu as pltpu

F32 = jnp.float32
BF16 = jnp.bfloat16

N_HEADS = 16
HEAD_DIM = 64
D_ATT = N_HEADS * HEAD_DIM
CONV_W = 3
EPS = 1e-6
NEG_INF = -1e30
LOG2_E = 1.4426950408889634

LANES = 128
SUBLANES = 8
HEAD_PAD = LANES
N_PARTS = 3
BF16_ROWS = 16
V_ROWS = HEAD_DIM + BF16_ROWS
VMEM_LIMIT_BYTES = 56 * 2**20

ROW_TILE = 1024
CONV_ROW_TILE = 256
Q_TILE = 512
FLASH_SEQS = 2
PAGES_PER_STEP = 16
_HIGHEST = lax.Precision.HIGHEST


def _params(n_grid_dims):
    return pltpu.CompilerParams(
        dimension_semantics=("arbitrary",) * n_grid_dims,
        vmem_limit_bytes=VMEM_LIMIT_BYTES)


def _resident(shape):
    zeros = (0,) * len(shape)
    return pl.BlockSpec(shape, lambda *_: zeros, pipeline_mode=pl.Buffered(1))


def _rows(tm, width):
    return pl.BlockSpec((tm, width), lambda i: (i, 0))


def _dot(a, b):
    return jnp.dot(a, b, preferred_element_type=F32)


def _rmsnorm(x, g):
    return x * lax.rsqrt(jnp.mean(x * x, axis=-1, keepdims=True) + EPS) * g


def _silu(z):
    return z * jax.nn.sigmoid(z)


def _log_sigmoid(x):
    return jnp.minimum(x, 0.0) - jnp.log1p(jnp.exp(-jnp.abs(x)))


def _ple_residual(x1, p_ref, wg_ref, wp_ref):
    gate = jax.nn.sigmoid(_dot(x1.astype(BF16), wg_ref[...]))
    proj = _dot(p_ref[...].astype(BF16), wp_ref[...])
    return x1 + gate * proj


def _seq_index(idx, seq_len):
    if seq_len & (seq_len - 1) == 0:
        shift = seq_len.bit_length() - 1
        return lax.shift_right_logical(idx, shift), idx & (seq_len - 1)
    return idx // seq_len, idx % seq_len


def _split_bf16(c):
    hi = c.astype(BF16)
    r1 = c - hi.astype(F32)
    mid = r1.astype(BF16)
    lo = (r1 - mid.astype(F32)).astype(BF16)
    return hi, mid, lo


def _bias_pieces_t(ct):
    tm = ct.shape[1]
    ones_row = jnp.where(lax.broadcasted_iota(jnp.int32, (BF16_ROWS, tm), 0) == 0, 1.0, 0.0).astype(BF16)
    return jnp.concatenate(list(_split_bf16(ct)) + [ones_row], axis=0)


def _prefix_sum_lanes(x):
    n = x.shape[1]
    lane = lax.broadcasted_iota(jnp.int32, x.shape, 1)
    shift = 1
    while shift < n:
        x = x + jnp.where(lane >= shift, pltpu.roll(x, shift, axis=1), 0.0)
        shift *= 2
    return x


def _conv_core(x_ref, p_ref, g_ref, win_ref, cw_ref, wout_ref, wg_ref, wp_ref, xo_ref, ubuf,
               fix_first_rows):
    tm, d = x_ref.shape
    x = x_ref[...]
    h = _rmsnorm(x, g_ref[...]).astype(BF16)
    proj = _dot(h, win_ref[...])
    b_gate, c_gate, xin, z = (proj[:, k * d:(k + 1) * d] for k in range(4))
    u = c_gate * xin
    ubuf[SUBLANES:SUBLANES + tm, :] = u
    u1 = ubuf[SUBLANES - 1:SUBLANES - 1 + tm, :]
    u2 = ubuf[SUBLANES - 2:SUBLANES - 2 + tm, :]
    u1, u2 = fix_first_rows(u1, u2)
    cw = cw_ref[...]
    conv = cw[0:1, :] * u2 + cw[1:2, :] * u1 + cw[2:3, :] * u
    y = b_gate * conv * _silu(z)
    x1 = x + _dot(y.astype(BF16), wout_ref[...])
    xo_ref[...] = _ple_residual(x1, p_ref, wg_ref, wp_ref)
    return u


def _conv_long_kernel(x_ref, p_ref, g_ref, win_ref, cw_ref, wout_ref, wg_ref, wp_ref,
                      xo_ref, tail_ref, ubuf, *, tiles_per_seq):
    tm, d = x_ref.shape
    first = (pl.program_id(0) % tiles_per_seq) == 0

    @pl.when(first)
    def _():
        ubuf[0:SUBLANES, :] = jnp.zeros((SUBLANES, d), F32)

    @pl.when(jnp.logical_not(first))
    def _():
        ubuf[0:SUBLANES, :] = ubuf[tm:tm + SUBLANES, :]

    _conv_core(x_ref, p_ref, g_ref, win_ref, cw_ref, wout_ref, wg_ref, wp_ref, xo_ref, ubuf,
               lambda u1, u2: (u1, u2))
    tail_ref[0] = ubuf[tm:tm + SUBLANES, :]


def _conv_short_kernel(x_ref, p_ref, s1_ref, s2_ref, g_ref, win_ref, cw_ref, wout_ref, wg_ref,
                       wp_ref, xo_ref, u_ref, ubuf, *, seq_len):
    tm, d = x_ref.shape
    ubuf[0:SUBLANES, :] = jnp.zeros((SUBLANES, d), F32)
    _, t = _seq_index(lax.broadcasted_iota(jnp.int32, (tm, d), 0), seq_len)

    def fix(u1, u2):
        return (jnp.where(t < 1, s1_ref[...], u1), jnp.where(t < 2, s2_ref[...], u2))

    u_ref[...] = _conv_core(x_ref, p_ref, g_ref, win_ref, cw_ref, wout_ref, wg_ref, wp_ref,
                            xo_ref, ubuf, fix)


def _layer_rows(tm, p_all, layer):
    return pl.BlockSpec((None, tm, p_all.shape[2]), lambda i: (layer, i, 0))


def _conv_layer(x, p_all, layer, state, w, seq_len):
    n, d = x.shape
    n_seq = n // seq_len
    weights = (w["g"], w["w_in"], w["conv_w"], w["w_out"], w["w_gate"], w["w_proj"])
    w_specs = [_resident(a.shape) for a in weights]
    if state is None:
        tm = min(CONV_ROW_TILE, seq_len)
        assert seq_len % tm == 0 and tm % SUBLANES == 0
        x_new, tails = pl.pallas_call(
            functools.partial(_conv_long_kernel, tiles_per_seq=seq_len // tm),
            grid=(n // tm,),
            in_specs=[_rows(tm, d), _layer_rows(tm, p_all, layer)] + w_specs,
            out_specs=[_rows(tm, d), pl.BlockSpec((1, SUBLANES, d), lambda i: (i, 0, 0))],
            out_shape=[jax.ShapeDtypeStruct((n, d), F32),
                       jax.ShapeDtypeStruct((n // tm, SUBLANES, d), F32)],
            scratch_shapes=[pltpu.VMEM((tm + SUBLANES, d), F32)],
            compiler_params=_params(1),
            name="conv_layer_prompt",
        )(x, p_all, *weights)
        last = tails.reshape(n_seq, seq_len // tm, SUBLANES, d)[:, -1]
        return x_new, last[:, SUBLANES - (CONV_W - 1):, :]
    tm = n
    assert tm % seq_len == 0 and seq_len >= CONV_W - 1
    zeros = jnp.zeros((n_seq, seq_len, d), F32)
    s1 = zeros.at[:, 0].set(state[:, 1]).reshape(n, d)
    s2 = zeros.at[:, 0].set(state[:, 0]).at[:, 1].set(state[:, 1]).reshape(n, d)
    x_new, u = pl.pallas_call(
        functools.partial(_conv_short_kernel, seq_len=seq_len),
        grid=(n // tm,),
        in_specs=[_rows(tm, d), _layer_rows(tm, p_all, layer), _rows(tm, d), _rows(tm, d)] + w_specs,
        out_specs=[_rows(tm, d), _rows(tm, d)],
        out_shape=[jax.ShapeDtypeStruct((n, d), F32), jax.ShapeDtypeStruct((n, d), F32)],
        scratch_shapes=[pltpu.VMEM((tm + SUBLANES, d), F32)],
        compiler_params=_params(1),
        name="conv_layer_sample",
    )(x, p_all, s1, s2, *weights)
    return x_new, u.reshape(n_seq, seq_len, d)[:, seq_len - (CONV_W - 1):, :]


def _cumsum_rows(vals, seq_len):
    tm = vals.shape[0]
    r = lax.broadcasted_iota(jnp.int32, (tm, tm), 0)
    c = lax.broadcasted_iota(jnp.int32, (tm, tm), 1)
    if seq_len < tm:
        lower = jnp.where(c <= r, _seq_index(c, seq_len)[0], -1)
        tri = jnp.where(lower == _seq_index(r, seq_len)[0], 1.0, 0.0)
    else:
        tri = jnp.where(c <= r, 1.0, 0.0)
    return jnp.dot(tri.astype(F32), vals, preferred_element_type=F32, precision=_HIGHEST)


def _dot_nt(a, b):
    return lax.dot_general(a, b, (((1,), (1,)), ((), ())), preferred_element_type=F32)


def _kv_long_kernel(x_ref, g_ref, wka_ref, wkt_ref, wvt_ref, wft_ref, bfc_ref,
                    kt_ref, vt_ref, logft_ref, ct_ref, ka_ref, vtb_ref, carry, *, tiles_per_seq):
    tm = x_ref.shape[0]

    @pl.when((pl.program_id(0) % tiles_per_seq) == 0)
    def _():
        carry[...] = jnp.zeros(carry.shape, F32)

    h = _rmsnorm(x_ref[...], g_ref[...]).astype(BF16)
    logf_t = _log_sigmoid(_dot_nt(wft_ref[...], h) + bfc_ref[...])
    logft_ref[0] = logf_t
    run = carry[:, 0:1]
    blocks = []
    for b in range(tm // LANES):
        blocks.append(_prefix_sum_lanes(logf_t[:, b * LANES:(b + 1) * LANES]) + run)
        run = blocks[-1][:, LANES - 1:LANES]
    carry[...] = jnp.broadcast_to(run, carry.shape)
    c2_t = jnp.concatenate(blocks, axis=1) * LOG2_E
    ct_ref[0] = c2_t
    pieces_t = _bias_pieces_t(c2_t).astype(F32)
    pieces = jnp.concatenate([pieces_t, jnp.zeros((LANES - pieces_t.shape[0], tm), F32)], axis=0).T
    lhs = jnp.concatenate([h, pieces.astype(BF16)], axis=1)
    ka_ref[...] = _dot(lhs, wka_ref[...]).astype(BF16)
    kt_ref[0] = _dot_nt(wkt_ref[...], h)
    vt = _dot_nt(wvt_ref[...], h)
    vt_ref[0] = vt
    ones_row = jnp.where(lax.broadcasted_iota(jnp.int32, (BF16_ROWS, tm), 0) == 0, 1.0, 0.0).astype(BF16)
    for hd in range(N_HEADS):
        vtb_ref[0, hd * V_ROWS:hd * V_ROWS + HEAD_DIM, :] = (
            vt[hd * HEAD_DIM:(hd + 1) * HEAD_DIM, :].astype(BF16))
        vtb_ref[0, hd * V_ROWS + HEAD_DIM:(hd + 1) * V_ROWS, :] = ones_row


def _kv_short_kernel(x_ref, g_ref, wa_ref, bf_ref, k_ref, v_ref, logf_ref, c_ref, *, seq_len):
    h = _rmsnorm(x_ref[...], g_ref[...]).astype(BF16)
    r = _dot(h, wa_ref[...])
    k_ref[...] = r[:, :D_ATT]
    v_ref[...] = r[:, D_ATT:2 * D_ATT]
    logf = _log_sigmoid(r[:, 2 * D_ATT:] + bf_ref[...])
    logf_ref[...] = logf[:, :N_HEADS]
    c_ref[...] = _cumsum_rows(logf, seq_len)


def _shared_kv(x, w, n_seq, seq_len, long_seq):
    n, d = x.shape
    if long_seq:
        tm = min(ROW_TILE, seq_len)
        assert seq_len % tm == 0
        tps = seq_len // tm
        weights = (w["g_kv"], w["w_k_aug"], w["w_kt"], w["w_vt"], w["w_ft"], w["b_f_col"])
        by_seq = lambda rows: pl.BlockSpec((1, rows, tm), lambda i: (i // tps, 0, i % tps))
        kt, vt, logft, ct, ka, vtb = pl.pallas_call(
            functools.partial(_kv_long_kernel, tiles_per_seq=tps),
            grid=(n // tm,),
            in_specs=[_rows(tm, d)] + [_resident(a.shape) for a in weights],
            out_specs=[by_seq(D_ATT), by_seq(D_ATT), by_seq(N_HEADS), by_seq(N_HEADS),
                       _rows(tm, N_HEADS * HEAD_PAD), by_seq(N_HEADS * V_ROWS)],
            out_shape=[jax.ShapeDtypeStruct((n_seq, D_ATT, seq_len), F32),
                       jax.ShapeDtypeStruct((n_seq, D_ATT, seq_len), F32),
                       jax.ShapeDtypeStruct((n_seq, N_HEADS, seq_len), F32),
                       jax.ShapeDtypeStruct((n_seq, N_HEADS, seq_len), F32),
                       jax.ShapeDtypeStruct((n, N_HEADS * HEAD_PAD), BF16),
                       jax.ShapeDtypeStruct((n_seq, N_HEADS * V_ROWS, seq_len), BF16)],
            scratch_shapes=[pltpu.VMEM((N_HEADS, LANES), F32)],
            compiler_params=_params(1),
            name="shared_kv_prompt",
        )(x, *weights)
        to_std = lambda a: a.reshape(n_seq, N_HEADS, HEAD_DIM, seq_len).transpose(0, 3, 1, 2)
        return dict(k=to_std(kt), v=to_std(vt), logf=logft.transpose(0, 2, 1), c=ct, ka=ka, vtb=vtb)
    tm = n
    assert tm % seq_len == 0
    weights = (w["g_kv"], w["w_kvf"], w["b_f"])
    k, v, logf, c = pl.pallas_call(
        functools.partial(_kv_short_kernel, seq_len=seq_len),
        grid=(n // tm,),
        in_specs=[_rows(tm, d)] + [_resident(a.shape) for a in weights],
        out_specs=[_rows(tm, D_ATT), _rows(tm, D_ATT), _rows(tm, N_HEADS), _rows(tm, LANES)],
        out_shape=[jax.ShapeDtypeStruct((n, D_ATT), F32), jax.ShapeDtypeStruct((n, D_ATT), F32),
                   jax.ShapeDtypeStruct((n, N_HEADS), F32), jax.ShapeDtypeStruct((n, LANES), F32)],
        compiler_params=_params(1),
        name="shared_kv_sample",
    )(x, *weights)
    std = lambda a: a.reshape(n_seq, seq_len, N_HEADS, HEAD_DIM)
    return dict(k=std(k), v=std(v), logf=logf.reshape(n_seq, seq_len, N_HEADS), c=c, k2=k, v2=v)


def _attn_in_prompt_kernel(x_ref, ct_ref, g_ref, wqt_ref, wz_ref, place_ref, qt_ref, sz_ref):
    tm = x_ref.shape[0]
    h = _rmsnorm(x_ref[...], g_ref[...]).astype(BF16)
    q_t = _dot_nt(wqt_ref[...], h)
    bias_rows = _dot(place_ref[...], _bias_pieces_t(ct_ref[0])).astype(BF16)
    no_rows = jnp.zeros((HEAD_PAD - HEAD_DIM - BF16_ROWS, tm), BF16)
    for hd in range(N_HEADS):
        r0 = hd * HEAD_PAD
        qt_ref[0, r0:r0 + HEAD_DIM, :] = q_t[hd * HEAD_DIM:(hd + 1) * HEAD_DIM, :].astype(BF16)
        qt_ref[0, r0 + HEAD_DIM:r0 + HEAD_DIM + BF16_ROWS, :] = bias_rows[hd * BF16_ROWS:(hd + 1) * BF16_ROWS, :]
        qt_ref[0, r0 + HEAD_DIM + BF16_ROWS:r0 + HEAD_PAD, :] = no_rows
    sz_ref[...] = _silu(_dot(h, wz_ref[...]))


def _attn_in_sample_kernel(x_ref, g_ref, wq_ref, wz_ref, q_ref, sz_ref):
    h = _rmsnorm(x_ref[...], g_ref[...]).astype(BF16)
    q_ref[...] = _dot(h, wq_ref[...])
    sz_ref[...] = _silu(_dot(h, wz_ref[...]))


def _attn_in(x, c, w, n_seq, seq_len, long_seq):
    n, d = x.shape
    if long_seq:
        tm = min(ROW_TILE, seq_len)
        assert seq_len % tm == 0
        tps = seq_len // tm
        weights = (w["g"], w["w_qt"], w["w_z"], w["q_bias_place"])
        return pl.pallas_call(
            _attn_in_prompt_kernel,
            grid=(n // tm,),
            in_specs=[_rows(tm, d), pl.BlockSpec((1, N_HEADS, tm), lambda i: (i // tps, 0, i % tps))]
            + [_resident(a.shape) for a in weights],
            out_specs=[pl.BlockSpec((1, N_HEADS * HEAD_PAD, tm), lambda i: (i // tps, 0, i % tps)),
                       _rows(tm, D_ATT)],
            out_shape=[jax.ShapeDtypeStruct((n_seq, N_HEADS * HEAD_PAD, seq_len), BF16),
                       jax.ShapeDtypeStruct((n, D_ATT), F32)],
            compiler_params=_params(1),
            name="attn_in_prompt",
        )(x, c, *weights)
    tm = min(ROW_TILE, n)
    weights = (w["g"], w["w_q"], w["w_z"])
    return pl.pallas_call(
        _attn_in_sample_kernel,
        grid=(n // tm,),
        in_specs=[_rows(tm, d)] + [_resident(a.shape) for a in weights],
        out_specs=[_rows(tm, D_ATT), _rows(tm, D_ATT)],
        out_shape=[jax.ShapeDtypeStruct((n, D_ATT), F32), jax.ShapeDtypeStruct((n, D_ATT), F32)],
        compiler_params=_params(1),
        name="attn_in_sample",
    )(x, *weights)


def _attn_out_kernel(x_ref, o_ref, sz_ref, p_ref, wout_ref, wg_ref, wp_ref, gf_ref, out_ref, *,
                     final):
    y = (o_ref[...] * sz_ref[...]).astype(BF16)
    x1 = x_ref[...] + _dot(y, wout_ref[...])
    x_new = _ple_residual(x1, p_ref, wg_ref, wp_ref)
    out_ref[...] = _rmsnorm(x_new, gf_ref[...]) if final else x_new


def _attn_out(x, o, sz, p_all, layer, w, g_final, final):
    n, d = x.shape
    tm = min(ROW_TILE, n)
    weights = (w["w_out"], w["w_gate"], w["w_proj"], g_final)
    return pl.pallas_call(
        functools.partial(_attn_out_kernel, final=final),
        grid=(n // tm,),
        in_specs=[_rows(tm, d), _rows(tm, D_ATT), _rows(tm, D_ATT), _layer_rows(tm, p_all, layer)]
        + [_resident(a.shape) for a in weights],
        out_specs=_rows(tm, d),
        out_shape=jax.ShapeDtypeStruct((n, d), F32),
        compiler_params=_params(1),
        name="attn_out",
    )(x, o, sz, p_all, *weights)


def _flash_kernel(qt_ref, ka_ref, vt_ref, o_ref, s_scr, mt_scr, m_scr, acc_scr, *, tile):
    i = pl.program_id(2)
    chains = [(b, hh) for b in range(qt_ref.shape[0]) for hh in range(2)]
    qts = [qt_ref[b, hh * HEAD_PAD:(hh + 1) * HEAD_PAD, :] for b, hh in chains]

    def logits(j, slot, mask):
        off = pl.multiple_of(j * tile, tile)
        for c, (b, hh) in enumerate(chains):
            s = _dot(ka_ref[b, pl.ds(off, tile), hh * HEAD_PAD:(hh + 1) * HEAD_PAD], qts[c])
            if mask is not None:
                s = jnp.where(mask, s, NEG_INF)
            s_scr[slot, c] = s
            mt_scr[slot, c] = jnp.max(s, axis=0, keepdims=True)

    def fold(j, slot):
        off = pl.multiple_of(j * tile, tile)
        for c, (b, hh) in enumerate(chains):
            m = m_scr[c]
            m_new = jnp.maximum(m, mt_scr[slot, c])
            p = jnp.exp2(s_scr[slot, c] - m_new).astype(BF16)
            vt = vt_ref[b, hh * V_ROWS:(hh + 1) * V_ROWS, pl.ds(off, tile)]
            acc_scr[c] = jnp.exp2(m - m_new) * acc_scr[c] + _dot(vt, p)
            m_scr[c] = m_new

    m_scr[...] = jnp.full(m_scr.shape, NEG_INF, F32)
    acc_scr[...] = jnp.zeros(acc_scr.shape, F32)
    key = lax.broadcasted_iota(jnp.int32, (tile, tile), 0)
    qry = lax.broadcasted_iota(jnp.int32, (tile, tile), 1)
    logits(i, 0, key <= qry)
    n_tiles = i + 1
    visit = lambda k: jnp.where(k == 0, i, k - 1)

    def two_steps(u, _):
        k = 2 * u
        logits(visit(k + 1), 1, None)
        fold(visit(k), 0)
        logits(visit(k + 2), 0, None)
        fold(visit(k + 1), 1)
        return 0

    n_pairs = (n_tiles - 1) // 2
    lax.fori_loop(0, n_pairs, two_steps, 0)
    k = 2 * n_pairs

    @pl.when(n_tiles - k == 2)
    def _():
        logits(visit(k + 1), 1, None)
        fold(visit(k), 0)
        fold(visit(k + 1), 1)

    @pl.when(n_tiles - k == 1)
    def _():
        fold(visit(k), 0)

    for b in range(qt_ref.shape[0]):
        o_t = jnp.concatenate([acc_scr[c, :HEAD_DIM, :] / acc_scr[c, HEAD_DIM:HEAD_DIM + 1, :]
                               for c in (2 * b, 2 * b + 1)], axis=0)
        o_ref[b] = o_t.T


def _prompt_attention(qt, ka, vtb, n_seq, seq_len):
    tile = min(Q_TILE, seq_len)
    assert seq_len % tile == 0
    pair = 2 * HEAD_PAD
    ka3 = ka.reshape(n_seq, seq_len, N_HEADS * HEAD_PAD)
    nb = FLASH_SEQS if n_seq % FLASH_SEQS == 0 else 1
    n_chains = 2 * nb
    o = pl.pallas_call(
        functools.partial(_flash_kernel, tile=tile),
        grid=(n_seq // nb, N_HEADS // 2, seq_len // tile),
        in_specs=[pl.BlockSpec((nb, pair, tile), lambda b, hp, i: (b, hp, i)),
                  pl.BlockSpec((nb, seq_len, pair), lambda b, hp, i: (b, 0, hp)),
                  pl.BlockSpec((nb, 2 * V_ROWS, seq_len), lambda b, hp, i: (b, hp, 0))],
        out_specs=pl.BlockSpec((nb, tile, 2 * HEAD_DIM), lambda b, hp, i: (b, i, hp)),
        out_shape=jax.ShapeDtypeStruct((n_seq, seq_len, D_ATT), F32),
        scratch_shapes=[pltpu.VMEM((2, n_chains, tile, tile), F32),
                        pltpu.VMEM((2, n_chains, 1, tile), F32),
                        pltpu.VMEM((n_chains, 1, tile), F32),
                        pltpu.VMEM((n_chains, V_ROWS, tile), F32)],
        compiler_params=_params(3),
        name="prompt_attention",
    )(qt, ka3, vtb)
    return o.reshape(n_seq * seq_len, D_ATT)


def _online_update(s, pv_of, m_s, l_s, acc_s):
    m_prev = m_s[...]
    m_new = jnp.maximum(m_prev, jnp.max(s, axis=-1, keepdims=True))
    alpha = jnp.exp(m_prev - m_new)
    p = jnp.exp(s - m_new)
    l_s[...] = alpha * l_s[...] + jnp.sum(p, axis=-1, keepdims=True)
    acc_s[...] = alpha * acc_s[...] + pv_of(p.astype(BF16))
    m_s[...] = m_new


def _suffix_sum_lanes(x):
    n = x.shape[1]
    lane = lax.broadcasted_iota(jnp.int32, x.shape, 1)
    y = jnp.where(lane + 1 < n, pltpu.roll(x, n - 1, axis=1), 0.0)
    shift = 1
    while shift < n:
        y = y + jnp.where(lane + shift < n, pltpu.roll(y, n - shift, axis=1), 0.0)
        shift *= 2
    return y


def _per_head_rows(bias, seq_len):
    return jnp.concatenate(
        [jnp.broadcast_to(bias[h:h + 1, :], (seq_len, bias.shape[1])) for h in range(N_HEADS)], axis=0)


def _sample_attn_kernel(pt_ref, q_ref, cnq_ref, kn_ref, vn_ref, cnk_ref, *refs, ppc):
    del pt_ref
    k_refs, v_refs, f_refs = refs[:ppc], refs[ppc:2 * ppc], refs[2 * ppc:3 * ppc]
    o_ref, qbd_s, m_s, l_s, acc_s, tot_s = refs[3 * ppc:]
    seq_len = q_ref.shape[0]
    rows = N_HEADS * seq_len
    step = pl.program_id(1)
    row_head = _seq_index(lax.broadcasted_iota(jnp.int32, (rows, D_ATT), 0), seq_len)[0]
    own_head = row_head == _seq_index(lax.broadcasted_iota(jnp.int32, (rows, D_ATT), 1), HEAD_DIM)[0]

    @pl.when(step == 0)
    def _():
        q_rows = jnp.concatenate([q_ref[...]] * N_HEADS, axis=0)
        qbd_s[...] = jnp.where(own_head, q_rows, 0.0).astype(BF16)
        m_s[...] = jnp.full(m_s.shape, NEG_INF, F32)
        l_s[...] = jnp.zeros(l_s.shape, F32)
        acc_s[...] = jnp.zeros(acc_s.shape, F32)
        tot_s[...] = jnp.zeros(tot_s.shape, F32)

    page = k_refs[0].shape[3]
    kc = jnp.concatenate([r[0].reshape(D_ATT, page) for r in k_refs], axis=1).astype(BF16)
    vc = jnp.concatenate([r[0].reshape(D_ATT, page) for r in v_refs], axis=1).astype(BF16)

    after = tot_s[...]
    rest_pages = [None] * ppc
    for pg in reversed(range(ppc)):
        logf_pg = f_refs[pg][0]
        rest_pages[pg] = _suffix_sum_lanes(logf_pg) + after
        after = after + jnp.sum(logf_pg, axis=1, keepdims=True)
    tot_s[...] = after
    rest = jnp.concatenate(rest_pages, axis=1)

    s = _dot(qbd_s[...], kc) + _per_head_rows(rest, seq_len) + cnq_ref[0]
    _online_update(s, lambda p: _dot_nt(p, vc), m_s, l_s, acc_s)

    @pl.when(step == pl.num_programs(1) - 1)
    def _():
        no_keys = jnp.zeros((LANES - seq_len, D_ATT), F32)
        kn = jnp.concatenate([kn_ref[...], no_keys], axis=0).astype(BF16)
        vn = jnp.concatenate([vn_ref[...], no_keys], axis=0).astype(BF16)
        s2 = _dot_nt(qbd_s[...], kn) - _per_head_rows(cnk_ref[0], seq_len) + cnq_ref[0]
        t = _seq_index(lax.broadcasted_iota(jnp.int32, s2.shape, 0), seq_len)[1]
        sp = lax.broadcasted_iota(jnp.int32, s2.shape, 1)
        s2 = jnp.where(sp <= t, s2, NEG_INF)
        _online_update(s2, lambda p: _dot(p, vn), m_s, l_s, acc_s)
        o_blocks = jnp.where(own_head, acc_s[...] / l_s[...], 0.0)
        o_ref[...] = functools.reduce(
            lambda a, b: a + b, [o_blocks[h * seq_len:(h + 1) * seq_len, :] for h in range(N_HEADS)])


def _sample_attention(q, k_new, v_new, c_new, cache_k, cache_v, cache_logf, page_table, seq_len):
    n = q.shape[0]
    n_seq = n // seq_len
    n_pages = page_table.shape[1]
    page = cache_k.shape[1]
    ppc = PAGES_PER_STEP
    assert n_pages % ppc == 0 and seq_len <= LANES
    n_chunks = n_pages // ppc

    cache_kt = cache_k.transpose(0, 2, 3, 1)
    cache_vt = cache_v.transpose(0, 2, 3, 1)
    cache_ft = cache_logf.transpose(0, 2, 1)

    rows = N_HEADS * seq_len
    cn = c_new[:, :N_HEADS].reshape(n_seq, seq_len, N_HEADS).transpose(0, 2, 1)
    cnq = cn.reshape(n_seq, rows, 1)
    cnk = jnp.pad(cn, ((0, 0), (0, 0), (0, LANES - seq_len)))

    def page_map(j):
        return lambda b, c, pt: (pt[b, (n_chunks - 1 - c) * ppc + j], 0, 0, 0)

    def logf_map(j):
        return lambda b, c, pt: (pt[b, (n_chunks - 1 - c) * ppc + j], 0, 0)

    seq_rows = pl.BlockSpec((seq_len, D_ATT), lambda b, c, pt: (b, 0))
    kv_page = lambda j: pl.BlockSpec((1, N_HEADS, HEAD_DIM, page), page_map(j))
    return pl.pallas_call(
        functools.partial(_sample_attn_kernel, ppc=ppc),
        grid_spec=pltpu.PrefetchScalarGridSpec(
            num_scalar_prefetch=1,
            grid=(n_seq, n_chunks),
            in_specs=[seq_rows, pl.BlockSpec((1, rows, 1), lambda b, c, pt: (b, 0, 0)),
                      seq_rows, seq_rows,
                      pl.BlockSpec((1, N_HEADS, LANES), lambda b, c, pt: (b, 0, 0))]
            + [kv_page(j) for j in range(ppc)] + [kv_page(j) for j in range(ppc)]
            + [pl.BlockSpec((1, N_HEADS, page), logf_map(j)) for j in range(ppc)],
            out_specs=seq_rows,
            scratch_shapes=[pltpu.VMEM((rows, D_ATT), BF16),
                            pltpu.VMEM((rows, 1), F32), pltpu.VMEM((rows, 1), F32),
                            pltpu.VMEM((rows, D_ATT), F32),
                            pltpu.VMEM((N_HEADS, 1), F32)]),
        out_shape=jax.ShapeDtypeStruct((n, D_ATT), F32),
        compiler_params=_params(2),
        name="sample_attention",
    )(page_table, q, cnq, k_new, v_new,
      cnk, *([cache_kt] * ppc), *([cache_vt] * ppc), *([cache_ft] * ppc))


def _pad_heads(w):
    d = w.shape[0]
    w = w.reshape(d, N_HEADS, HEAD_DIM)
    return jnp.pad(w, ((0, 0), (0, 0), (0, HEAD_PAD - HEAD_DIM))).reshape(d, N_HEADS * HEAD_PAD)


def _k_bias_placement():
    rows = np.zeros((LANES, N_HEADS * HEAD_PAD), np.float32)
    for h in range(N_HEADS):
        for part in range(N_PARTS):
            rows[part * N_HEADS + h, h * HEAD_PAD + HEAD_DIM + part] = -1.0
            rows[N_PARTS * N_HEADS, h * HEAD_PAD + HEAD_DIM + N_PARTS + part] = 1.0
    return jnp.asarray(rows, BF16)


def _q_bias_placement():
    place = np.zeros((N_HEADS * BF16_ROWS, N_PARTS * N_HEADS + BF16_ROWS), np.float32)
    for h in range(N_HEADS):
        for part in range(N_PARTS):
            place[h * BF16_ROWS + part, N_PARTS * N_HEADS] = 1.0
            place[h * BF16_ROWS + N_PARTS + part, part * N_HEADS + h] = 1.0
    return jnp.asarray(place, BF16)


def _prepare_weights(norm_a, w_in_a, conv_w_a, w_out_a, norm_kv, w_kv, b_f, norm_b, w_in_b, w_out_b,
                     w_ple_proj, w_ple_gate, norm_f):
    n_a, n_b = w_in_a.shape[0], w_in_b.shape[0]
    scale = HEAD_DIM ** -0.5
    k_rows = _k_bias_placement()
    q_place = _q_bias_placement()
    ple = lambda i: dict(w_gate=w_ple_gate[i].astype(BF16), w_proj=w_ple_proj[i].astype(BF16))
    conv = [dict(g=norm_a[i][None, :], w_in=w_in_a[i].astype(BF16), conv_w=conv_w_a[i],
                 w_out=w_out_a[i].astype(BF16), **ple(i)) for i in range(n_a)]
    w_k, w_v, w_f = w_kv[:, :D_ATT], w_kv[:, D_ATT:2 * D_ATT], w_kv[:, 2 * D_ATT:]
    w_f_pad = jnp.pad(w_f, ((0, 0), (0, LANES - N_HEADS)))
    kv = dict(g_kv=norm_kv[None, :],
              w_kvf=jnp.concatenate([w_k, w_v, w_f_pad], axis=1).astype(BF16),
              b_f=jnp.pad(b_f, (0, LANES - N_HEADS))[None, :],
              b_f_col=b_f[:, None],
              w_k_aug=jnp.concatenate([_pad_heads(w_k).astype(BF16), k_rows], axis=0),
              w_kt=w_k.T.astype(BF16), w_vt=w_v.T.astype(BF16), w_ft=w_f.T.astype(BF16))
    attn = []
    for j in range(n_b):
        w_q = w_in_b[j][:, :D_ATT] * scale
        attn.append(dict(g=norm_b[j][None, :], w_q=w_q.astype(BF16),
                         w_qt=(w_q * LOG2_E).T.astype(BF16), q_bias_place=q_place,
                         w_z=w_in_b[j][:, D_ATT:].astype(BF16),
                         w_out=w_out_b[j].astype(BF16), **ple(n_a + j)))
    return conv, kv, attn, norm_f[None, :]


def _trunk(x3, p4, conv_state, weights, attend):
    conv_w, kv_w, attn_w, g_final = weights
    n_seq, seq_len, d = x3.shape
    n = n_seq * seq_len
    long_seq = conv_state is None
    x = x3.reshape(n, d)
    p = p4.reshape(p4.shape[0], n, p4.shape[-1])
    new_conv = []
    for i, w in enumerate(conv_w):
        x, st = _conv_layer(x, p, i, None if long_seq else conv_state[i], w, seq_len)
        new_conv.append(st)
    kv = _shared_kv(x, kv_w, n_seq, seq_len, long_seq)
    for j, w in enumerate(attn_w):
        q, sz = _attn_in(x, kv["c"], w, n_seq, seq_len, long_seq)
        o = attend(q, kv)
        x = _attn_out(x, o, sz, p, len(conv_w) + j, w, g_final, final=(j == len(attn_w) - 1))
    return x.reshape(n_seq, seq_len, d), jnp.stack(new_conv), kv["k"], kv["v"], kv["logf"]


def kernel(x_prompt, x_sample, p_prompt, p_sample, state_conv, cache_k, cache_v, cache_logf, page_table, norm_a, w_in_a, conv_w_a, w_out_a, norm_kv, w_kv, b_f, norm_b, w_in_b, w_out_b, w_ple_proj, w_ple_gate, norm_f):
    assert conv_w_a.shape[1] == CONV_W and w_kv.shape[1] == 2 * D_ATT + N_HEADS
    weights = _prepare_weights(norm_a, w_in_a, conv_w_a, w_out_a, norm_kv, w_kv, b_f, norm_b, w_in_b,
                               w_out_b, w_ple_proj, w_ple_gate, norm_f)
    n_prompt, prompt_len = x_prompt.shape[:2]
    sample_len = x_sample.shape[1]

    def prompt_attend(qt, kv):
        return _prompt_attention(qt, kv["ka"], kv["vtb"], n_prompt, prompt_len)

    def sample_attend(q, kv):
        return _sample_attention(q, kv["k2"], kv["v2"], kv["c"], cache_k, cache_v, cache_logf,
                                 page_table, sample_len)

    y_p, conv_p, k_p, v_p, logf_p = _trunk(x_prompt, p_prompt, None, weights, prompt_attend)
    y_s, conv_s, k_s, v_s, logf_s = _trunk(x_sample, p_sample, state_conv, weights, sample_attend)
    return (y_p, y_s, conv_p, k_p, v_p, logf_p, conv_s, k_s, v_s, logf_s)
```

```python
import functools

import numpy as np
import jax
import jax.numpy as jnp
from jax import lax
from jax.experimental import pallas as pl
from jax.experimental.pallas import tpu as pltpu

F32 = jnp.float32
BF16 = jnp.bfloat16

N_HEADS = 16
HEAD_DIM = 64
D_ATT = N_HEADS * HEAD_DIM
CONV_W = 3
EPS = 1e-6
NEG_INF = -1e30
LOG2_E = 1.4426950408889634

LANES = 128
SUBLANES = 8
HEAD_PAD = LANES
N_PARTS = 3
BF16_ROWS = 16
V_ROWS = HEAD_DIM + BF16_ROWS
VMEM_LIMIT_BYTES = 56 * 2**20

ROW_TILE = 1024
CONV_ROW_TILE = 256
Q_TILE = 512
FLASH_SEQS = 2
PAGES_PER_STEP = 16
_HIGHEST = lax.Precision.HIGHEST


def _params(n_grid_dims):
    return pltpu.CompilerParams(
        dimension_semantics=("arbitrary",) * n_grid_dims,
        vmem_limit_bytes=VMEM_LIMIT_BYTES)


def _resident(shape):
    zeros = (0,) * len(shape)
    return pl.BlockSpec(shape, lambda *_: zeros, pipeline_mode=pl.Buffered(1))


def _rows(tm, width):
    return pl.BlockSpec((tm, width), lambda i: (i, 0))


def _dot(a, b):
    return jnp.dot(a, b, preferred_element_type=F32)


def _rmsnorm(x, g):
    return x * lax.rsqrt(jnp.mean(x * x, axis=-1, keepdims=True) + EPS) * g


def _silu(z):
    return z * jax.nn.sigmoid(z)


def _log_sigmoid(x):
    return jnp.minimum(x, 0.0) - jnp.log1p(jnp.exp(-jnp.abs(x)))


def _ple_residual(x1, p_ref, wg_ref, wp_ref):
    gate = jax.nn.sigmoid(_dot(x1.astype(BF16), wg_ref[...]))
    proj = _dot(p_ref[...].astype(BF16), wp_ref[...])
    return x1 + gate * proj


def _seq_index(idx, seq_len):
    if seq_len & (seq_len - 1) == 0:
        shift = seq_len.bit_length() - 1
        return lax.shift_right_logical(idx, shift), idx & (seq_len - 1)
    return idx // seq_len, idx % seq_len


def _split_bf16(c):
    hi = c.astype(BF16)
    r1 = c - hi.astype(F32)
    mid = r1.astype(BF16)
    lo = (r1 - mid.astype(F32)).astype(BF16)
    return hi, mid, lo


def _bias_pieces_t(ct):
    tm = ct.shape[1]
    ones_row = jnp.where(lax.broadcasted_iota(jnp.int32, (BF16_ROWS, tm), 0) == 0, 1.0, 0.0).astype(BF16)
    return jnp.concatenate(list(_split_bf16(ct)) + [ones_row], axis=0)


def _prefix_sum_lanes(x):
    n = x.shape[1]
    lane = lax.broadcasted_iota(jnp.int32, x.shape, 1)
    shift = 1
    while shift < n:
        x = x + jnp.where(lane >= shift, pltpu.roll(x, shift, axis=1), 0.0)
        shift *= 2
    return x


def _conv_core(x_ref, p_ref, g_ref, win_ref, cw_ref, wout_ref, wg_ref, wp_ref, xo_ref, ubuf,
               fix_first_rows):
    tm, d = x_ref.shape
    x = x_ref[...]
    h = _rmsnorm(x, g_ref[...]).astype(BF16)
    proj = _dot(h, win_ref[...])
    b_gate, c_gate, xin, z = (proj[:, k * d:(k + 1) * d] for k in range(4))
    u = c_gate * xin
    ubuf[SUBLANES:SUBLANES + tm, :] = u
    u1 = ubuf[SUBLANES - 1:SUBLANES - 1 + tm, :]
    u2 = ubuf[SUBLANES - 2:SUBLANES - 2 + tm, :]
    u1, u2 = fix_first_rows(u1, u2)
    cw = cw_ref[...]
    conv = cw[0:1, :] * u2 + cw[1:2, :] * u1 + cw[2:3, :] * u
    y = b_gate * conv * _silu(z)
    x1 = x + _dot(y.astype(BF16), wout_ref[...])
    xo_ref[...] = _ple_residual(x1, p_ref, wg_ref, wp_ref)
    return u


def _conv_long_kernel(x_ref, p_ref, g_ref, win_ref, cw_ref, wout_ref, wg_ref, wp_ref,
                      xo_ref, tail_ref, ubuf, *, tiles_per_seq):
    tm, d = x_ref.shape
    first = (pl.program_id(0) % tiles_per_seq) == 0

    @pl.when(first)
    def _():
        ubuf[0:SUBLANES, :] = jnp.zeros((SUBLANES, d), F32)

    @pl.when(jnp.logical_not(first))
    def _():
        ubuf[0:SUBLANES, :] = ubuf[tm:tm + SUBLANES, :]

    _conv_core(x_ref, p_ref, g_ref, win_ref, cw_ref, wout_ref, wg_ref, wp_ref, xo_ref, ubuf,
               lambda u1, u2: (u1, u2))
    tail_ref[0] = ubuf[tm:tm + SUBLANES, :]


def _conv_short_kernel(x_ref, p_ref, s1_ref, s2_ref, g_ref, win_ref, cw_ref, wout_ref, wg_ref,
                       wp_ref, xo_ref, u_ref, ubuf, *, seq_len):
    tm, d = x_ref.shape
    ubuf[0:SUBLANES, :] = jnp.zeros((SUBLANES, d), F32)
    _, t = _seq_index(lax.broadcasted_iota(jnp.int32, (tm, d), 0), seq_len)

    def fix(u1, u2):
        return (jnp.where(t < 1, s1_ref[...], u1), jnp.where(t < 2, s2_ref[...], u2))

    u_ref[...] = _conv_core(x_ref, p_ref, g_ref, win_ref, cw_ref, wout_ref, wg_ref, wp_ref,
                            xo_ref, ubuf, fix)


def _layer_rows(tm, p_all, layer):
    return pl.BlockSpec((None, tm, p_all.shape[2]), lambda i: (layer, i, 0))


def _conv_layer(x, p_all, layer, state, w, seq_len):
    n, d = x.shape
    n_seq = n // seq_len
    weights = (w["g"], w["w_in"], w["conv_w"], w["w_out"], w["w_gate"], w["w_proj"])
    w_specs = [_resident(a.shape) for a in weights]
    if state is None:
        tm = min(CONV_ROW_TILE, seq_len)
        assert seq_len % tm == 0 and tm % SUBLANES == 0
        x_new, tails = pl.pallas_call(
            functools.partial(_conv_long_kernel, tiles_per_seq=seq_len // tm),
            grid=(n // tm,),
            in_specs=[_rows(tm, d), _layer_rows(tm, p_all, layer)] + w_specs,
            out_specs=[_rows(tm, d), pl.BlockSpec((1, SUBLANES, d), lambda i: (i, 0, 0))],
            out_shape=[jax.ShapeDtypeStruct((n, d), F32),
                       jax.ShapeDtypeStruct((n // tm, SUBLANES, d), F32)],
            scratch_shapes=[pltpu.VMEM((tm + SUBLANES, d), F32)],
            compiler_params=_params(1),
            name="conv_layer_prompt",
        )(x, p_all, *weights)
        last = tails.reshape(n_seq, seq_len // tm, SUBLANES, d)[:, -1]
        return x_new, last[:, SUBLANES - (CONV_W - 1):, :]
    tm = n
    assert tm % seq_len == 0 and seq_len >= CONV_W - 1
    zeros = jnp.zeros((n_seq, seq_len, d), F32)
    s1 = zeros.at[:, 0].set(state[:, 1]).reshape(n, d)
    s2 = zeros.at[:, 0].set(state[:, 0]).at[:, 1].set(state[:, 1]).reshape(n, d)
    x_new, u = pl.pallas_call(
        functools.partial(_conv_short_kernel, seq_len=seq_len),
        grid=(n // tm,),
        in_specs=[_rows(tm, d), _layer_rows(tm, p_all, layer), _rows(tm, d), _rows(tm, d)] + w_specs,
        out_specs=[_rows(tm, d), _rows(tm, d)],
        out_shape=[jax.ShapeDtypeStruct((n, d), F32), jax.ShapeDtypeStruct((n, d), F32)],
        scratch_shapes=[pltpu.VMEM((tm + SUBLANES, d), F32)],
        compiler_params=_params(1),
        name="conv_layer_sample",
    )(x, p_all, s1, s2, *weights)
    return x_new, u.reshape(n_seq, seq_len, d)[:, seq_len - (CONV_W - 1):, :]


def _cumsum_rows(vals, seq_len):
    tm = vals.shape[0]
    r = lax.broadcasted_iota(jnp.int32, (tm, tm), 0)
    c = lax.broadcasted_iota(jnp.int32, (tm, tm), 1)
    if seq_len < tm:
        lower = jnp.where(c <= r, _seq_index(c, seq_len)[0], -1)
        tri = jnp.where(lower == _seq_index(r, seq_len)[0], 1.0, 0.0)
    else:
        tri = jnp.where(c <= r, 1.0, 0.0)
    return jnp.dot(tri.astype(F32), vals, preferred_element_type=F32, precision=_HIGHEST)


def _dot_nt(a, b):
    return lax.dot_general(a, b, (((1,), (1,)), ((), ())), preferred_element_type=F32)


def _kv_long_kernel(x_ref, g_ref, wka_ref, wkt_ref, wvt_ref, wft_ref, bfc_ref,
                    kt_ref, vt_ref, logft_ref, ct_ref, ka_ref, vtb_ref, carry, *, tiles_per_seq):
    tm = x_ref.shape[0]

    @pl.when((pl.program_id(0) % tiles_per_seq) == 0)
    def _():
        carry[...] = jnp.zeros(carry.shape, F32)

    h = _rmsnorm(x_ref[...], g_ref[...]).astype(BF16)
    logf_t = _log_sigmoid(_dot_nt(wft_ref[...], h) + bfc_ref[...])
    logft_ref[0] = logf_t
    run = carry[:, 0:1]
    blocks = []
    for b in range(tm // LANES):
        blocks.append(_prefix_sum_lanes(logf_t[:, b * LANES:(b + 1) * LANES]) + run)
        run = blocks[-1][:, LANES - 1:LANES]
    carry[...] = jnp.broadcast_to(run, carry.shape)
    c2_t = jnp.concatenate(blocks, axis=1) * LOG2_E
    ct_ref[0] = c2_t
    pieces_t = _bias_pieces_t(c2_t).astype(F32)
    pieces = jnp.concatenate([pieces_t, jnp.zeros((LANES - pieces_t.shape[0], tm), F32)], axis=0).T
    lhs = jnp.concatenate([h, pieces.astype(BF16)], axis=1)
    ka_ref[...] = _dot(lhs, wka_ref[...]).astype(BF16)
    kt_ref[0] = _dot_nt(wkt_ref[...], h)
    vt = _dot_nt(wvt_ref[...], h)
    vt_ref[0] = vt
    ones_row = jnp.where(lax.broadcasted_iota(jnp.int32, (BF16_ROWS, tm), 0) == 0, 1.0, 0.0).astype(BF16)
    for hd in range(N_HEADS):
        vtb_ref[0, hd * V_ROWS:hd * V_ROWS + HEAD_DIM, :] = (
            vt[hd * HEAD_DIM:(hd + 1) * HEAD_DIM, :].astype(BF16))
        vtb_ref[0, hd * V_ROWS + HEAD_DIM:(hd + 1) * V_ROWS, :] = ones_row


def _kv_short_kernel(x_ref, g_ref, wa_ref, bf_ref, k_ref, v_ref, logf_ref, c_ref, *, seq_len):
    h = _rmsnorm(x_ref[...], g_ref[...]).astype(BF16)
    r = _dot(h, wa_ref[...])
    k_ref[...] = r[:, :D_ATT]
    v_ref[...] = r[:, D_ATT:2 * D_ATT]
    logf = _log_sigmoid(r[:, 2 * D_ATT:] + bf_ref[...])
    logf_ref[...] = logf[:, :N_HEADS]
    c_ref[...] = _cumsum_rows(logf, seq_len)


def _shared_kv(x, w, n_seq, seq_len, long_seq):
    n, d = x.shape
    if long_seq:
        tm = min(ROW_TILE, seq_len)
        assert seq_len % tm == 0
        tps = seq_len // tm
        weights = (w["g_kv"], w["w_k_aug"], w["w_kt"], w["w_vt"], w["w_ft"], w["b_f_col"])
        by_seq = lambda rows: pl.BlockSpec((1, rows, tm), lambda i: (i // tps, 0, i % tps))
        kt, vt, logft, ct, ka, vtb = pl.pallas_call(
            functools.partial(_kv_long_kernel, tiles_per_seq=tps),
            grid=(n // tm,),
            in_specs=[_rows(tm, d)] + [_resident(a.shape) for a in weights],
            out_specs=[by_seq(D_ATT), by_seq(D_ATT), by_seq(N_HEADS), by_seq(N_HEADS),
                       _rows(tm, N_HEADS * HEAD_PAD), by_seq(N_HEADS * V_ROWS)],
            out_shape=[jax.ShapeDtypeStruct((n_seq, D_ATT, seq_len), F32),
                       jax.ShapeDtypeStruct((n_seq, D_ATT, seq_len), F32),
                       jax.ShapeDtypeStruct((n_seq, N_HEADS, seq_len), F32),
                       jax.ShapeDtypeStruct((n_seq, N_HEADS, seq_len), F32),
                       jax.ShapeDtypeStruct((n, N_HEADS * HEAD_PAD), BF16),
                       jax.ShapeDtypeStruct((n_seq, N_HEADS * V_ROWS, seq_len), BF16)],
            scratch_shapes=[pltpu.VMEM((N_HEADS, LANES), F32)],
            compiler_params=_params(1),
            name="shared_kv_prompt",
        )(x, *weights)
        to_std = lambda a: a.reshape(n_seq, N_HEADS, HEAD_DIM, seq_len).transpose(0, 3, 1, 2)
        return dict(k=to_std(kt), v=to_std(vt), logf=logft.transpose(0, 2, 1), c=ct, ka=ka, vtb=vtb)
    tm = n
    assert tm % seq_len == 0
    weights = (w["g_kv"], w["w_kvf"], w["b_f"])
    k, v, logf, c = pl.pallas_call(
        functools.partial(_kv_short_kernel, seq_len=seq_len),
        grid=(n // tm,),
        in_specs=[_rows(tm, d)] + [_resident(a.shape) for a in weights],
        out_specs=[_rows(tm, D_ATT), _rows(tm, D_ATT), _rows(tm, N_HEADS), _rows(tm, LANES)],
        out_shape=[jax.ShapeDtypeStruct((n, D_ATT), F32), jax.ShapeDtypeStruct((n, D_ATT), F32),
                   jax.ShapeDtypeStruct((n, N_HEADS), F32), jax.ShapeDtypeStruct((n, LANES), F32)],
        compiler_params=_params(1),
        name="shared_kv_sample",
    )(x, *weights)
    std = lambda a: a.reshape(n_seq, seq_len, N_HEADS, HEAD_DIM)
    return dict(k=std(k), v=std(v), logf=logf.reshape(n_seq, seq_len, N_HEADS), c=c, k2=k, v2=v)


def _attn_in_prompt_kernel(x_ref, ct_ref, g_ref, wqt_ref, wz_ref, place_ref, qt_ref, sz_ref):
    tm = x_ref.shape[0]
    h = _rmsnorm(x_ref[...], g_ref[...]).astype(BF16)
    q_t = _dot_nt(wqt_ref[...], h)
    bias_rows = _dot(place_ref[...], _bias_pieces_t(ct_ref[0])).astype(BF16)
    no_rows = jnp.zeros((HEAD_PAD - HEAD_DIM - BF16_ROWS, tm), BF16)
    for hd in range(N_HEADS):
        r0 = hd * HEAD_PAD
        qt_ref[0, r0:r0 + HEAD_DIM, :] = q_t[hd * HEAD_DIM:(hd + 1) * HEAD_DIM, :].astype(BF16)
        qt_ref[0, r0 + HEAD_DIM:r0 + HEAD_DIM + BF16_ROWS, :] = bias_rows[hd * BF16_ROWS:(hd + 1) * BF16_ROWS, :]
        qt_ref[0, r0 + HEAD_DIM + BF16_ROWS:r0 + HEAD_PAD, :] = no_rows
    sz_ref[...] = _silu(_dot(h, wz_ref[...])).astype(sz_ref.dtype)


def _attn_in_sample_kernel(x_ref, g_ref, wq_ref, wz_ref, q_ref, sz_ref):
    h = _rmsnorm(x_ref[...], g_ref[...]).astype(BF16)
    q_ref[...] = _dot(h, wq_ref[...])
    sz_ref[...] = _silu(_dot(h, wz_ref[...]))


def _attn_in(x, c, w, n_seq, seq_len, long_seq):
    n, d = x.shape
    if long_seq:
        tm = min(ROW_TILE, seq_len)
        assert seq_len % tm == 0
        tps = seq_len // tm
        weights = (w["g"], w["w_qt"], w["w_z"], w["q_bias_place"])
        return pl.pallas_call(
            _attn_in_prompt_kernel,
            grid=(n // tm,),
            in_specs=[_rows(tm, d), pl.BlockSpec((1, N_HEADS, tm), lambda i: (i // tps, 0, i % tps))]
            + [_resident(a.shape) for a in weights],
            out_specs=[pl.BlockSpec((1, N_HEADS * HEAD_PAD, tm), lambda i: (i // tps, 0, i % tps)),
                       _rows(tm, D_ATT)],
            out_shape=[jax.ShapeDtypeStruct((n_seq, N_HEADS * HEAD_PAD, seq_len), BF16),
                       jax.ShapeDtypeStruct((n, D_ATT), BF16)],
            compiler_params=_params(1),
            name="attn_in_prompt",
        )(x, c, *weights)
    tm = min(ROW_TILE, n)
    weights = (w["g"], w["w_q"], w["w_z"])
    return pl.pallas_call(
        _attn_in_sample_kernel,
        grid=(n // tm,),
        in_specs=[_rows(tm, d)] + [_resident(a.shape) for a in weights],
        out_specs=[_rows(tm, D_ATT), _rows(tm, D_ATT)],
        out_shape=[jax.ShapeDtypeStruct((n, D_ATT), F32), jax.ShapeDtypeStruct((n, D_ATT), F32)],
        compiler_params=_params(1),
        name="attn_in_sample",
    )(x, *weights)


def _attn_out_kernel(x_ref, o_ref, sz_ref, p_ref, wout_ref, wg_ref, wp_ref, gf_ref, out_ref, *,
                     final):
    y = (o_ref[...] * sz_ref[...].astype(F32)).astype(BF16)
    x1 = x_ref[...] + _dot(y, wout_ref[...])
    x_new = _ple_residual(x1, p_ref, wg_ref, wp_ref)
    out_ref[...] = _rmsnorm(x_new, gf_ref[...]) if final else x_new


def _attn_out(x, o, sz, p_all, layer, w, g_final, final):
    n, d = x.shape
    tm = min(ROW_TILE, n)
    weights = (w["w_out"], w["w_gate"], w["w_proj"], g_final)
    return pl.pallas_call(
        functools.partial(_attn_out_kernel, final=final),
        grid=(n // tm,),
        in_specs=[_rows(tm, d), _rows(tm, D_ATT), _rows(tm, D_ATT), _layer_rows(tm, p_all, layer)]
        + [_resident(a.shape) for a in weights],
        out_specs=_rows(tm, d),
        out_shape=jax.ShapeDtypeStruct((n, d), F32),
        compiler_params=_params(1),
        name="attn_out",
    )(x, o, sz, p_all, *weights)


def _flash_kernel(qt_ref, ka_ref, vt_ref, o_ref, s_scr, mt_scr, m_scr, acc_scr, *, tile):
    i = pl.program_id(2)
    chains = [(b, hh) for b in range(qt_ref.shape[0]) for hh in range(2)]
    qts = [qt_ref[b, hh * HEAD_PAD:(hh + 1) * HEAD_PAD, :] for b, hh in chains]

    def logits(j, slot, mask):
        off = pl.multiple_of(j * tile, tile)
        for c, (b, hh) in enumerate(chains):
            s = _dot(ka_ref[b, pl.ds(off, tile), hh * HEAD_PAD:(hh + 1) * HEAD_PAD], qts[c])
            if mask is not None:
                s = jnp.where(mask, s, NEG_INF)
            s_scr[slot, c] = s
            mt_scr[slot, c] = jnp.max(s, axis=0, keepdims=True)

    def fold(j, slot):
        off = pl.multiple_of(j * tile, tile)
        for c, (b, hh) in enumerate(chains):
            m = m_scr[c]
            m_new = jnp.maximum(m, mt_scr[slot, c])
            p = jnp.exp2(s_scr[slot, c] - m_new).astype(BF16)
            vt = vt_ref[b, hh * V_ROWS:(hh + 1) * V_ROWS, pl.ds(off, tile)]
            acc_scr[c] = jnp.exp2(m - m_new) * acc_scr[c] + _dot(vt, p)
            m_scr[c] = m_new

    m_scr[...] = jnp.full(m_scr.shape, NEG_INF, F32)
    acc_scr[...] = jnp.zeros(acc_scr.shape, F32)
    key = lax.broadcasted_iota(jnp.int32, (tile, tile), 0)
    qry = lax.broadcasted_iota(jnp.int32, (tile, tile), 1)
    logits(i, 0, key <= qry)
    n_tiles = i + 1
    visit = lambda k: jnp.where(k == 0, i, k - 1)

    def two_steps(u, _):
        k = 2 * u
        logits(visit(k + 1), 1, None)
        fold(visit(k), 0)
        logits(visit(k + 2), 0, None)
        fold(visit(k + 1), 1)
        return 0

    n_pairs = (n_tiles - 1) // 2
    lax.fori_loop(0, n_pairs, two_steps, 0)
    k = 2 * n_pairs

    @pl.when(n_tiles - k == 2)
    def _():
        logits(visit(k + 1), 1, None)
        fold(visit(k), 0)
        fold(visit(k + 1), 1)

    @pl.when(n_tiles - k == 1)
    def _():
        fold(visit(k), 0)

    for b in range(qt_ref.shape[0]):
        o_t = jnp.concatenate([acc_scr[c, :HEAD_DIM, :] / acc_scr[c, HEAD_DIM:HEAD_DIM + 1, :]
                               for c in (2 * b, 2 * b + 1)], axis=0)
        o_ref[b] = o_t.T


def _prompt_attention(qt, ka, vtb, n_seq, seq_len):
    tile = min(Q_TILE, seq_len)
    assert seq_len % tile == 0
    pair = 2 * HEAD_PAD
    ka3 = ka.reshape(n_seq, seq_len, N_HEADS * HEAD_PAD)
    nb = FLASH_SEQS if n_seq % FLASH_SEQS == 0 else 1
    n_chains = 2 * nb
    o = pl.pallas_call(
        functools.partial(_flash_kernel, tile=tile),
        grid=(n_seq // nb, N_HEADS // 2, seq_len // tile),
        in_specs=[pl.BlockSpec((nb, pair, tile), lambda b, hp, i: (b, hp, i)),
                  pl.BlockSpec((nb, seq_len, pair), lambda b, hp, i: (b, 0, hp)),
                  pl.BlockSpec((nb, 2 * V_ROWS, seq_len), lambda b, hp, i: (b, hp, 0))],
        out_specs=pl.BlockSpec((nb, tile, 2 * HEAD_DIM), lambda b, hp, i: (b, i, hp)),
        out_shape=jax.ShapeDtypeStruct((n_seq, seq_len, D_ATT), F32),
        scratch_shapes=[pltpu.VMEM((2, n_chains, tile, tile), F32),
                        pltpu.VMEM((2, n_chains, 1, tile), F32),
                        pltpu.VMEM((n_chains, 1, tile), F32),
                        pltpu.VMEM((n_chains, V_ROWS, tile), F32)],
        compiler_params=_params(3),
        name="prompt_attention",
    )(qt, ka3, vtb)
    return o.reshape(n_seq * seq_len, D_ATT)


def _online_update(s, pv_of, m_s, l_s, acc_s):
    m_prev = m_s[...]
    m_new = jnp.maximum(m_prev, jnp.max(s, axis=-1, keepdims=True))
    alpha = jnp.exp(m_prev - m_new)
    p = jnp.exp(s - m_new)
    l_s[...] = alpha * l_s[...] + jnp.sum(p, axis=-1, keepdims=True)
    acc_s[...] = alpha * acc_s[...] + pv_of(p.astype(BF16))
    m_s[...] = m_new


def _suffix_sum_lanes(x):
    n = x.shape[1]
    lane = lax.broadcasted_iota(jnp.int32, x.shape, 1)
    y = jnp.where(lane + 1 < n, pltpu.roll(x, n - 1, axis=1), 0.0)
    shift = 1
    while shift < n:
        y = y + jnp.where(lane + shift < n, pltpu.roll(y, n - shift, axis=1), 0.0)
        shift *= 2
    return y


def _per_head_rows(bias, seq_len):
    return jnp.concatenate(
        [jnp.broadcast_to(bias[h:h + 1, :], (seq_len, bias.shape[1])) for h in range(N_HEADS)], axis=0)


def _sample_attn_kernel(pt_ref, q_ref, cnq_ref, kn_ref, vn_ref, cnk_ref, *refs, ppc):
    del pt_ref
    k_refs, v_refs, f_refs = refs[:ppc], refs[ppc:2 * ppc], refs[2 * ppc:3 * ppc]
    o_ref, qbd_s, m_s, l_s, acc_s, tot_s = refs[3 * ppc:]
    seq_len = q_ref.shape[0]
    rows = N_HEADS * seq_len
    step = pl.program_id(1)
    row_head = _seq_index(lax.broadcasted_iota(jnp.int32, (rows, D_ATT), 0), seq_len)[0]
    own_head = row_head == _seq_index(lax.broadcasted_iota(jnp.int32, (rows, D_ATT), 1), HEAD_DIM)[0]

    @pl.when(step == 0)
    def _():
        q_rows = jnp.concatenate([q_ref[...]] * N_HEADS, axis=0)
        qbd_s[...] = jnp.where(own_head, q_rows, 0.0).astype(BF16)
        m_s[...] = jnp.full(m_s.shape, NEG_INF, F32)
        l_s[...] = jnp.zeros(l_s.shape, F32)
        acc_s[...] = jnp.zeros(acc_s.shape, F32)
        tot_s[...] = jnp.zeros(tot_s.shape, F32)

    page = k_refs[0].shape[3]
    kc = jnp.concatenate([r[0].reshape(D_ATT, page) for r in k_refs], axis=1).astype(BF16)
    vc = jnp.concatenate([r[0].reshape(D_ATT, page) for r in v_refs], axis=1).astype(BF16)

    after = tot_s[...]
    rest_pages = [None] * ppc
    for pg in reversed(range(ppc)):
        logf_pg = f_refs[pg][0]
        rest_pages[pg] = _suffix_sum_lanes(logf_pg) + after
        after = after + jnp.sum(logf_pg, axis=1, keepdims=True)
    tot_s[...] = after
    rest = jnp.concatenate(rest_pages, axis=1)

    s = _dot(qbd_s[...], kc) + _per_head_rows(rest, seq_len) + cnq_ref[0]
    _online_update(s, lambda p: _dot_nt(p, vc), m_s, l_s, acc_s)

    @pl.when(step == pl.num_programs(1) - 1)
    def _():
        no_keys = jnp.zeros((LANES - seq_len, D_ATT), F32)
        kn = jnp.concatenate([kn_ref[...], no_keys], axis=0).astype(BF16)
        vn = jnp.concatenate([vn_ref[...], no_keys], axis=0).astype(BF16)
        s2 = _dot_nt(qbd_s[...], kn) - _per_head_rows(cnk_ref[0], seq_len) + cnq_ref[0]
        t = _seq_index(lax.broadcasted_iota(jnp.int32, s2.shape, 0), seq_len)[1]
        sp = lax.broadcasted_iota(jnp.int32, s2.shape, 1)
        s2 = jnp.where(sp <= t, s2, NEG_INF)
        _online_update(s2, lambda p: _dot(p, vn), m_s, l_s, acc_s)
        o_blocks = jnp.where(own_head, acc_s[...] / l_s[...], 0.0)
        o_ref[...] = functools.reduce(
            lambda a, b: a + b, [o_blocks[h * seq_len:(h + 1) * seq_len, :] for h in range(N_HEADS)])


def _sample_attention(q, k_new, v_new, c_new, cache_k, cache_v, cache_logf, page_table, seq_len):
    n = q.shape[0]
    n_seq = n // seq_len
    n_pages = page_table.shape[1]
    page = cache_k.shape[1]
    ppc = PAGES_PER_STEP
    assert n_pages % ppc == 0 and seq_len <= LANES
    n_chunks = n_pages // ppc

    cache_kt = cache_k.transpose(0, 2, 3, 1)
    cache_vt = cache_v.transpose(0, 2, 3, 1)
    cache_ft = cache_logf.transpose(0, 2, 1)

    rows = N_HEADS * seq_len
    cn = c_new[:, :N_HEADS].reshape(n_seq, seq_len, N_HEADS).transpose(0, 2, 1)
    cnq = cn.reshape(n_seq, rows, 1)
    cnk = jnp.pad(cn, ((0, 0), (0, 0), (0, LANES - seq_len)))

    def page_map(j):
        return lambda b, c, pt: (pt[b, (n_chunks - 1 - c) * ppc + j], 0, 0, 0)

    def logf_map(j):
        return lambda b, c, pt: (pt[b, (n_chunks - 1 - c) * ppc + j], 0, 0)

    seq_rows = pl.BlockSpec((seq_len, D_ATT), lambda b, c, pt: (b, 0))
    kv_page = lambda j: pl.BlockSpec((1, N_HEADS, HEAD_DIM, page), page_map(j))
    return pl.pallas_call(
        functools.partial(_sample_attn_kernel, ppc=ppc),
        grid_spec=pltpu.PrefetchScalarGridSpec(
            num_scalar_prefetch=1,
            grid=(n_seq, n_chunks),
            in_specs=[seq_rows, pl.BlockSpec((1, rows, 1), lambda b, c, pt: (b, 0, 0)),
                      seq_rows, seq_rows,
                      pl.BlockSpec((1, N_HEADS, LANES), lambda b, c, pt: (b, 0, 0))]
            + [kv_page(j) for j in range(ppc)] + [kv_page(j) for j in range(ppc)]
            + [pl.BlockSpec((1, N_HEADS, page), logf_map(j)) for j in range(ppc)],
            out_specs=seq_rows,
            scratch_shapes=[pltpu.VMEM((rows, D_ATT), BF16),
                            pltpu.VMEM((rows, 1), F32), pltpu.VMEM((rows, 1), F32),
                            pltpu.VMEM((rows, D_ATT), F32),
                            pltpu.VMEM((N_HEADS, 1), F32)]),
        out_shape=jax.ShapeDtypeStruct((n, D_ATT), F32),
        compiler_params=_params(2),
        name="sample_attention",
    )(page_table, q, cnq, k_new, v_new,
      cnk, *([cache_kt] * ppc), *([cache_vt] * ppc), *([cache_ft] * ppc))


def _pad_heads(w):
    d = w.shape[0]
    w = w.reshape(d, N_HEADS, HEAD_DIM)
    return jnp.pad(w, ((0, 0), (0, 0), (0, HEAD_PAD - HEAD_DIM))).reshape(d, N_HEADS * HEAD_PAD)


def _k_bias_placement():
    rows = np.zeros((LANES, N_HEADS * HEAD_PAD), np.float32)
    for h in range(N_HEADS):
        for part in range(N_PARTS):
            rows[part * N_HEADS + h, h * HEAD_PAD + HEAD_DIM + part] = -1.0
            rows[N_PARTS * N_HEADS, h * HEAD_PAD + HEAD_DIM + N_PARTS + part] = 1.0
    return jnp.asarray(rows, BF16)


def _q_bias_placement():
    place = np.zeros((N_HEADS * BF16_ROWS, N_PARTS * N_HEADS + BF16_ROWS), np.float32)
    for h in range(N_HEADS):
        for part in range(N_PARTS):
            place[h * BF16_ROWS + part, N_PARTS * N_HEADS] = 1.0
            place[h * BF16_ROWS + N_PARTS + part, part * N_HEADS + h] = 1.0
    return jnp.asarray(place, BF16)


def _prepare_weights(norm_a, w_in_a, conv_w_a, w_out_a, norm_kv, w_kv, b_f, norm_b, w_in_b, w_out_b,
                     w_ple_proj, w_ple_gate, norm_f):
    n_a, n_b = w_in_a.shape[0], w_in_b.shape[0]
    scale = HEAD_DIM ** -0.5
    k_rows = _k_bias_placement()
    q_place = _q_bias_placement()
    ple = lambda i: dict(w_gate=w_ple_gate[i].astype(BF16), w_proj=w_ple_proj[i].astype(BF16))
    conv = [dict(g=norm_a[i][None, :], w_in=w_in_a[i].astype(BF16), conv_w=conv_w_a[i],
                 w_out=w_out_a[i].astype(BF16), **ple(i)) for i in range(n_a)]
    w_k, w_v, w_f = w_kv[:, :D_ATT], w_kv[:, D_ATT:2 * D_ATT], w_kv[:, 2 * D_ATT:]
    w_f_pad = jnp.pad(w_f, ((0, 0), (0, LANES - N_HEADS)))
    kv = dict(g_kv=norm_kv[None, :],
              w_kvf=jnp.concatenate([w_k, w_v, w_f_pad], axis=1).astype(BF16),
              b_f=jnp.pad(b_f, (0, LANES - N_HEADS))[None, :],
              b_f_col=b_f[:, None],
              w_k_aug=jnp.concatenate([_pad_heads(w_k).astype(BF16), k_rows], axis=0),
              w_kt=w_k.T.astype(BF16), w_vt=w_v.T.astype(BF16), w_ft=w_f.T.astype(BF16))
    attn = []
    for j in range(n_b):
        w_q = w_in_b[j][:, :D_ATT] * scale
        attn.append(dict(g=norm_b[j][None, :], w_q=w_q.astype(BF16),
                         w_qt=(w_q * LOG2_E).T.astype(BF16), q_bias_place=q_place,
                         w_z=w_in_b[j][:, D_ATT:].astype(BF16),
                         w_out=w_out_b[j].astype(BF16), **ple(n_a + j)))
    return conv, kv, attn, norm_f[None, :]


def _trunk(x3, p4, conv_state, weights, attend):
    conv_w, kv_w, attn_w, g_final = weights
    n_seq, seq_len, d = x3.shape
    n = n_seq * seq_len
    long_seq = conv_state is None
    x = x3.reshape(n, d)
    p = p4.reshape(p4.shape[0], n, p4.shape[-1])
    new_conv = []
    for i, w in enumerate(conv_w):
        x, st = _conv_layer(x, p, i, None if long_seq else conv_state[i], w, seq_len)
        new_conv.append(st)
    kv = _shared_kv(x, kv_w, n_seq, seq_len, long_seq)
    for j, w in enumerate(attn_w):
        q, sz = _attn_in(x, kv["c"], w, n_seq, seq_len, long_seq)
        o = attend(q, kv)
        x = _attn_out(x, o, sz, p, len(conv_w) + j, w, g_final, final=(j == len(attn_w) - 1))
    return x.reshape(n_seq, seq_len, d), jnp.stack(new_conv), kv["k"], kv["v"], kv["logf"]


def kernel(x_prompt, x_sample, p_prompt, p_sample, state_conv, cache_k, cache_v, cache_logf, page_table, norm_a, w_in_a, conv_w_a, w_out_a, norm_kv, w_kv, b_f, norm_b, w_in_b, w_out_b, w_ple_proj, w_ple_gate, norm_f):
    assert conv_w_a.shape[1] == CONV_W and w_kv.shape[1] == 2 * D_ATT + N_HEADS
    weights = _prepare_weights(norm_a, w_in_a, conv_w_a, w_out_a, norm_kv, w_kv, b_f, norm_b, w_in_b,
                               w_out_b, w_ple_proj, w_ple_gate, norm_f)
    n_prompt, prompt_len = x_prompt.shape[:2]
    sample_len = x_sample.shape[1]

    def prompt_attend(qt, kv):
        return _prompt_attention(qt, kv["ka"], kv["vtb"], n_prompt, prompt_len)

    def sample_attend(q, kv):
        return _sample_attention(q, kv["k2"], kv["v2"], kv["c"], cache_k, cache_v, cache_logf,
                                 page_table, sample_len)

    y_p, conv_p, k_p, v_p, logf_p = _trunk(x_prompt, p_prompt, None, weights, prompt_attend)
    y_s, conv_s, k_s, v_s, logf_s = _trunk(x_sample, p_sample, state_conv, weights, sample_attend)
    return (y_p, y_s, conv_p, k_p, v_p, logf_p, conv_s, k_s, v_s, logf_s)
```
